```python
import math
import jax
import jax.numpy as jnp
from jax import lax
import numpy as np

D_MODEL = 4096
BATCH = 1
SEQ = 16384
DEPTH = 4

GRID_W = 64
CTX_LEN = 256
N_BRANCH = 3
BRANCH_W = 3 * D_MODEL // 8
SSM_HEAD_DIM = 64
SSM_HEADS = BRANCH_W // SSM_HEAD_DIM
SSM_GROUPS = 4
SSM_HPG = SSM_HEADS // SSM_GROUPS
SSM_STATE = 128
CONV_W = 5
XBC_W = BRANCH_W + 2 * SSM_GROUPS * SSM_STATE
HGRN_VAL = 128
HGRN_HEADS = BRANCH_W // HGRN_VAL
HGRN_KEY = 64
HGRN_HK = HGRN_HEADS * HGRN_KEY
FOURIER_GROUPS = 4
GATE_RANK = 256
ADA_RANK = 256
D_FF = 4 * D_MODEL
CHUNK = 128
EPS = 1e-6
F_MIN = 1e-30
IN_SPLITS = (BRANCH_W, XBC_W, 2 * SSM_HEADS, HGRN_HK, 2 * HGRN_HK, BRANCH_W, BRANCH_W, BRANCH_W, GATE_RANK)
N_IN = sum(IN_SPLITS)

kernel_name = "hybrid_ssd_hgrn2_fnet_dit_trunk"


def rms_norm(x, g):
    xf = x.astype(jnp.float32)
    y = xf * lax.rsqrt(jnp.mean(xf * xf, axis=-1, keepdims=True) + EPS)
    return y.astype(x.dtype) * g


def modulate(h, shift, scale):
    return h * (1 + scale) + shift


def adaln(cond, down, up, bias):
    m = (jax.nn.silu(cond) @ down) @ up + bias
    return [t[:, None, :] for t in jnp.split(m, 6, axis=-1)]


def flip(t):
    return jnp.flip(t, axis=1)


def raster_to_colmajor(t, rows):
    b, L = t.shape[:2]
    rest = t.shape[2:]
    return t.reshape(b, rows, GRID_W, *rest).swapaxes(1, 2).reshape(b, L, *rest)


def colmajor_to_raster(t, rows):
    b, L = t.shape[:2]
    rest = t.shape[2:]
    return t.reshape(b, GRID_W, rows, *rest).swapaxes(1, 2).reshape(b, L, *rest)


def masked_exp(diff, mask):
    return jnp.where(mask, jnp.exp(jnp.where(mask, diff, 0.0)), 0.0)


def dw_conv(u, w, bias):
    y = lax.conv_general_dilated(u, w[:, None, :], window_strides=(1,),
                                 padding=[(CONV_W // 2, CONV_W // 2)],
                                 dimension_numbers=("NWC", "WIO", "NWC"),
                                 feature_group_count=u.shape[-1])
    return y + bias


def ssd_chunked(x, a, bm, cm, h0, with_output):
    b, L = x.shape[:2]
    nc = L // CHUNK
    ch = lambda t: t.reshape(b, nc, CHUNK, *t.shape[2:])
    x, a, bm, cm = ch(x), ch(a), ch(bm), ch(cm)
    acum = jnp.cumsum(a, axis=2)
    decay_end = jnp.exp(acum[:, :, -1:] - acum)
    states = jnp.einsum('bcsgn,bcsgrp->bcgrpn', bm, x * decay_end[..., None])
    chunk_decay = jnp.exp(acum[:, :, -1])

    def step(h, inp):
        s, d = inp
        return h * d[..., None, None] + s, h

    h_fin, h_in = lax.scan(step, h0, (jnp.moveaxis(states, 1, 0), jnp.moveaxis(chunk_decay, 1, 0)))
    if not with_output:
        return None, h_fin
    h_in = jnp.moveaxis(h_in, 0, 1)
    y_off = jnp.einsum('bclgn,bcgrpn->bclgrp', cm, h_in) * jnp.exp(acum)[..., None]
    tril = jnp.tril(jnp.ones((CHUNK, CHUNK), bool))
    diff = acum[:, :, :, None] - acum[:, :, None, :]
    decay = masked_exp(diff, tril[None, None, :, :, None, None])
    scores = jnp.einsum('bclgn,bcsgn->bclsg', cm, bm)
    y_diag = jnp.einsum('bclsgr,bcsgrp->bclgrp', scores[..., None] * decay, x)
    y = (y_diag + y_off).reshape(b, L, *x.shape[3:])
    return y, h_fin


def gla_chunked(q, k, v, logf, s0, with_output):
    b, L = k.shape[:2]
    nc = L // CHUNK
    chunks = lambda t: jnp.moveaxis(t.reshape(b, nc, CHUNK, *t.shape[2:]), 1, 0)
    tril = jnp.tril(jnp.ones((CHUNK, CHUNK), bool))

    def step(s, inp):
        qc, kc, vc, gc = inp
        gcum = jnp.cumsum(gc, axis=1)
        g_end = gcum[:, -1]
        s_new = s * jnp.exp(g_end)[..., None] + jnp.einsum(
            'bshk,bshv->bhkv', kc * jnp.exp(g_end[:, None] - gcum), vc)
        if not with_output:
            return s_new, None
        o_inter = jnp.einsum('blhk,bhkv->blhv', qc * jnp.exp(gcum), s)
        diff = gcum[:, :, None] - gcum[:, None, :]
        decay = masked_exp(diff, tril[None, :, :, None, None])
        attn = jnp.einsum('blshk,bshk->bhls', qc[:, :, None] * decay, kc)
        o_intra = jnp.einsum('bhls,bshv->blhv', attn, vc)
        return s_new, o_inter + o_intra

    s_fin, o = lax.scan(step, s0, (chunks(q), chunks(k), chunks(v), chunks(logf)))
    if not with_output:
        return None, s_fin
    o = jnp.moveaxis(o, 0, 1).reshape(b, L, *v.shape[2:])
    return o, s_fin


def mamba_branch(u_z, u_xbc, u_dt, p, h0_f, h0_b, with_output):
    f32 = jnp.float32
    b, L, _ = u_z.shape
    xbc = jax.nn.silu(dw_conv(u_xbc, p['conv_w'], p['conv_b']))
    xs, bm, cm = jnp.split(xbc, [BRANCH_W, BRANCH_W + SSM_GROUPS * SSM_STATE], axis=-1)
    xs = xs.astype(f32).reshape(b, L, SSM_GROUPS, SSM_HPG, SSM_HEAD_DIM)
    bm = bm.astype(f32).reshape(b, L, SSM_GROUPS, SSM_STATE)
    cm = cm.astype(f32).reshape(b, L, SSM_GROUPS, SSM_STATE)
    dt = jax.nn.softplus(u_dt.astype(f32).reshape(b, L, 2, SSM_GROUPS, SSM_HPG)
                         + p['dt_bias'].astype(f32).reshape(2, SSM_GROUPS, SSM_HPG))
    a = -jnp.exp(p['a_log'].astype(f32)).reshape(2, SSM_GROUPS, SSM_HPG)
    dt_f, dt_b = dt[:, :, 0], dt[:, :, 1]
    y_f, hf = ssd_chunked(xs * dt_f[..., None], dt_f * a[0], bm, cm, h0_f, with_output)
    y_b, hb = ssd_chunked(flip(xs * dt_b[..., None]), flip(dt_b * a[1]), flip(bm), flip(cm), h0_b, with_output)
    if not with_output:
        return None, hf, hb
    y = y_f + flip(y_b) + xs * p['d_skip'].astype(f32).reshape(SSM_GROUPS, SSM_HPG, 1)
    gsz = BRANCH_W // SSM_GROUPS
    y = y.reshape(b, L, SSM_GROUPS, gsz) * jax.nn.silu(u_z.astype(f32)).reshape(b, L, SSM_GROUPS, gsz)
    y = y * lax.rsqrt(jnp.mean(y * y, axis=-1, keepdims=True) + EPS)
    y = y.reshape(b, L, BRANCH_W) * p['ssm_norm']
    return y.astype(u_z.dtype), hf, hb


def hgrn_branch(u_q, u_f, u_i, u_g, p, s0_f, s0_b, with_output, rows):
    f32 = jnp.float32
    b, L, _ = u_q.shape
    q = jax.nn.silu(u_q.astype(f32)).reshape(b, L, HGRN_HEADS, HGRN_KEY)
    z = u_f.astype(f32).reshape(b, L, 2, HGRN_HEADS, HGRN_KEY)
    lb = p['lower_bound'].reshape(2, HGRN_HEADS, HGRN_KEY)
    f = lb + (1 - lb) * jax.nn.sigmoid(z)
    logf = jnp.log(jnp.maximum(f, F_MIN))
    k = (1 - lb) * jax.nn.sigmoid(-z)
    v = u_i.astype(f32).reshape(b, L, HGRN_HEADS, HGRN_VAL)
    if rows is not None:
        q, k, v, logf = [raster_to_colmajor(t, rows) for t in (q, k, v, logf)]
    o_f, sf = gla_chunked(q, k[:, :, 0], v, logf[:, :, 0], s0_f, with_output)
    o_b, sb = gla_chunked(flip(q), flip(k[:, :, 1]), flip(v), flip(logf[:, :, 1]), s0_b, with_output)
    if not with_output:
        return None, sf, sb
    o = o_f + flip(o_b)
    if rows is not None:
        o = colmajor_to_raster(o, rows)
    o = o * lax.rsqrt(jnp.mean(o * o, axis=-1, keepdims=True) + EPS) * p['hgrn_norm']
    o = o.reshape(b, L, BRANCH_W) * jax.nn.silu(u_g.astype(f32))
    return o.astype(u_q.dtype), sf, sb


def fourier_branch(u):
    b, L, W = u.shape
    uf = u.astype(jnp.float32).reshape(b, L, FOURIER_GROUPS, W // FOURIER_GROUPS)
    y = jnp.fft.fft2(uf, axes=(1, 3), norm="ortho").real
    return y.reshape(b, L, W).astype(u.dtype)


def mixer_branches(h, p, init, with_output, rows):
    idx = np.cumsum(IN_SPLITS)[:-1].tolist()
    u_z, u_xbc, u_dt, u_q, u_f, u_i, u_g, u_four, u_gate = jnp.split(h @ p['w_in'], idx, axis=-1)
    y_m, hm_f, hm_b = mamba_branch(u_z, u_xbc, u_dt, p, init[0], init[1], with_output)
    y_h, sh_f, sh_b = hgrn_branch(u_q, u_f, u_i, u_g, p, init[2], init[3], with_output, rows)
    states = (hm_f, hm_b, sh_f, sh_b)
    if not with_output:
        return None, None, states
    y_four = fourier_branch(u_four)
    return (y_m, y_h, y_four), u_gate, states


def merge_branches(branch_ys, u_gate, p):
    out = None
    for i, y in enumerate(branch_ys):
        gate = jax.nn.sigmoid(u_gate @ p['gate_up'][i] + p['gate_b'][i])
        term = gate * (y @ p['w_branch'][i])
        out = term if out is None else out + term
    return out @ p['w_out']


def sq_relu_mlp(h, up, down):
    return jnp.square(jax.nn.relu(h @ up)) @ down


def setup_inputs(seed: int = 0) -> dict:
    key = jax.random.key(seed)
    ks = jax.random.split(key, 24)
    f32 = jnp.float32
    nrm = lambda k, shape, scale: jax.random.normal(k, shape, f32) * scale
    D = D_MODEL
    dt0 = jnp.exp(jax.random.uniform(ks[11], (DEPTH, 2, SSM_HEADS), f32,
                                     minval=math.log(1e-3), maxval=math.log(1e-1)))
    return {
        "x": nrm(ks[0], (BATCH, SEQ, D), 1.0),
        "c": nrm(ks[1], (BATCH, D), 1.0),
        "ctx": nrm(ks[2], (BATCH, CTX_LEN, D), 1.0),
        "c_ctx": nrm(ks[3], (D,), 1.0),
        "ada_down": nrm(ks[4], (DEPTH, D, ADA_RANK), D ** -0.5),
        "ada_up": nrm(ks[5], (DEPTH, ADA_RANK, 6 * D), 0.5 * ADA_RANK ** -0.5),
        "ada_b": nrm(ks[6], (DEPTH, 6 * D), 0.02),
        "norm_g": 1.0 + nrm(ks[7], (DEPTH, 4, D), 0.05),
        "w_in": nrm(ks[8], (DEPTH, D, N_IN), D ** -0.5),
        "conv_w": nrm(ks[9], (DEPTH, CONV_W, XBC_W), CONV_W ** -0.5),
        "conv_b": nrm(ks[10], (DEPTH, XBC_W), 0.02),
        "ssm_dt_bias": dt0 + jnp.log(-jnp.expm1(-dt0)),
        "ssm_a_log": jnp.log(jax.random.uniform(ks[12], (DEPTH, 2, SSM_HEADS), f32, minval=1.0, maxval=16.0)),
        "ssm_d": 1.0 + nrm(ks[13], (DEPTH, SSM_HEADS), 0.1),
        "ssm_norm": 1.0 + nrm(ks[14], (DEPTH, BRANCH_W), 0.05),
        "hgrn_lb_logits": nrm(ks[15], (DEPTH, 2, HGRN_HK), 0.5),
        "hgrn_norm": 1.0 + nrm(ks[16], (DEPTH, HGRN_VAL), 0.05),
        "gate_up": nrm(ks[17], (DEPTH, N_BRANCH, GATE_RANK, D), GATE_RANK ** -0.5),
        "gate_b": nrm(ks[18], (DEPTH, N_BRANCH, D), 0.02),
        "w_branch": nrm(ks[19], (DEPTH, N_BRANCH, BRANCH_W, D), BRANCH_W ** -0.5),
        "w_out": nrm(ks[20], (DEPTH, D, D), D ** -0.5),
        "mlp_up": nrm(ks[21], (DEPTH, D, D_FF), D ** -0.5),
        "mlp_down": nrm(ks[22], (DEPTH, D_FF, D), D_FF ** -0.5),
    }


def reference(x, c, ctx, c_ctx, ada_down, ada_up, ada_b, norm_g, w_in, conv_w, conv_b,
              ssm_dt_bias, ssm_a_log, ssm_d, ssm_norm, hgrn_lb_logits, hgrn_norm,
              gate_up, gate_b, w_branch, w_out, mlp_up, mlp_down):
    f32 = jnp.float32
    b, L, _ = x.shape
    rows = L // GRID_W
    p_lb = jax.nn.softmax(hgrn_lb_logits.astype(f32), axis=0)
    lower_bounds = jnp.clip(jnp.cumsum(p_lb, axis=0) - p_lb[0], 0.0, 1.0)
    hm0 = jnp.zeros((ctx.shape[0], SSM_GROUPS, SSM_HPG, SSM_HEAD_DIM, SSM_STATE), f32)
    sh0 = jnp.zeros((ctx.shape[0], HGRN_HEADS, HGRN_KEY, HGRN_VAL), f32)
    zero_states = (hm0, hm0, sh0, sh0)
    for l in range(DEPTH):
        last = l == DEPTH - 1
        p = {
            'w_in': w_in[l], 'conv_w': conv_w[l], 'conv_b': conv_b[l],
            'dt_bias': ssm_dt_bias[l], 'a_log': ssm_a_log[l], 'd_skip': ssm_d[l],
            'ssm_norm': ssm_norm[l], 'lower_bound': lower_bounds[l], 'hgrn_norm': hgrn_norm[l],
            'gate_up': gate_up[l], 'gate_b': gate_b[l], 'w_branch': w_branch[l], 'w_out': w_out[l],
        }
        g = norm_g[l]
        sx1, cx1, gx1, sx2, cx2, gx2 = adaln(c, ada_down[l], ada_up[l], ada_b[l])
        sc1, cc1, gc1, sc2, cc2, gc2 = adaln(c_ctx[None], ada_down[l], ada_up[l], ada_b[l])
        hc = modulate(rms_norm(ctx, g[0]), sc1, cc1)
        hx = modulate(rms_norm(x, g[0]), sx1, cx1)
        yc, uc_gate, ctx_states = mixer_branches(hc, p, zero_states, not last, None)
        yx, ux_gate, _ = mixer_branches(hx, p, ctx_states, True, rows)
        x = x + gx1 * rms_norm(merge_branches(yx, ux_gate, p), g[1])
        hx = modulate(rms_norm(x, g[2]), sx2, cx2)
        x = x + gx2 * rms_norm(sq_relu_mlp(hx, mlp_up[l], mlp_down[l]), g[3])
        if not last:
            ctx = ctx + gc1 * rms_norm(merge_branches(yc, uc_gate, p), g[1])
            hc = modulate(rms_norm(ctx, g[2]), sc2, cc2)
            ctx = ctx + gc2 * rms_norm(sq_relu_mlp(hc, mlp_up[l], mlp_down[l]), g[3])
    return x
```

```python
import functools
import math

import numpy as np
import jax
import jax.numpy as jnp
from jax import lax
from jax.experimental import pallas as pl
from jax.experimental.pallas import tpu as pltpu

F32 = jnp.float32
BF16 = jnp.bfloat16

D_MODEL = 4096
DEPTH = 4
GRID_W = 64
BRANCH_W = 1536
SSM_HEAD_DIM = 64
SSM_HEADS = 24
SSM_GROUPS = 4
SSM_HPG = 6
SSM_STATE = 128
CONV_W = 5
XBC_W = BRANCH_W + 2 * SSM_GROUPS * SSM_STATE
HGRN_VAL = 128
HGRN_HEADS = 12
HGRN_KEY = 64
HGRN_HK = HGRN_HEADS * HGRN_KEY
HGRN_PAIRS = HGRN_HEADS // 2
FOURIER_GROUPS = 4
FOURIER_GW = BRANCH_W // FOURIER_GROUPS
GATE_RANK = 256
D_FF = 4 * D_MODEL
CHUNK = 128
EPS = 1e-6
F_MIN = 1e-30
IN_SPLITS = (BRANCH_W, XBC_W, 2 * SSM_HEADS, HGRN_HK, 2 * HGRN_HK, BRANCH_W, BRANCH_W, BRANCH_W, GATE_RANK)

LANES = 128
SUBLANES = 8
VMEM_LIMIT = 56 * 1024 * 1024

OFF_Z = 0
OFF_FOUR = OFF_Z + BRANCH_W
OFF_XBC = OFF_FOUR + BRANCH_W
OFF_Q = OFF_XBC + XBC_W
OFF_F = OFF_Q + HGRN_HK
OFF_I = OFF_F + 2 * HGRN_HK
OFF_G = OFF_I + BRANCH_W
OFF_GATE = OFF_G + BRANCH_W
OFF_DT = OFF_GATE + GATE_RANK
N_PAD = 11520
SUB = 16
NEG_BIG = -1e30


def _cparams(sem):
    return pltpu.CompilerParams(dimension_semantics=sem, vmem_limit_bytes=VMEM_LIMIT)


def _sigmoid(x):
    return 1.0 / (1.0 + jnp.exp(-x))


def _silu(x):
    return x * _sigmoid(x)


def _softplus(x):
    return jnp.maximum(x, 0.0) + jnp.log1p(jnp.exp(-jnp.abs(x)))


def _split3(x):
    hi = x.astype(BF16)
    r1 = x - hi.astype(F32)
    mid = r1.astype(BF16)
    lo = (r1 - mid.astype(F32)).astype(BF16)
    return hi, mid, lo


def _dot01_r(x, m01):
    hi, mid, lo = _split3(x)
    d = lambda a: jnp.dot(a, m01, preferred_element_type=F32)
    return d(hi) + d(mid) + d(lo)


def _dot01_l(m01, x):
    hi, mid, lo = _split3(x)
    d = lambda a: jnp.dot(m01, a, preferred_element_type=F32)
    return d(hi) + d(mid) + d(lo)


def _dot_nt(a, b):
    return lax.dot_general(a, b, (((1,), (1,)), ((), ())), preferred_element_type=F32)


def _rms_mod(x, g, shift, scale):
    ms = jnp.mean(x * x, axis=-1, keepdims=True)
    y = (x * lax.rsqrt(ms + EPS)) * g
    return y * (1.0 + scale) + shift


def _mm_small_kernel(a_ref, w_ref, b_ref, o_ref):
    o_ref[...] = jnp.dot(a_ref[...].astype(BF16), w_ref[...].astype(BF16),
                         preferred_element_type=F32) + b_ref[...]


def _mm_small(a, w, b, tn):
    m, k = a.shape
    n = w.shape[1]
    return pl.pallas_call(
        _mm_small_kernel,
        grid=(n // tn,),
        in_specs=[pl.BlockSpec((m, k), lambda j: (0, 0)),
                  pl.BlockSpec((k, tn), lambda j: (0, j)),
                  pl.BlockSpec((1, tn), lambda j: (0, j))],
        out_specs=pl.BlockSpec((m, tn), lambda j: (0, j)),
        out_shape=jax.ShapeDtypeStruct((m, n), F32),
        compiler_params=_cparams(("parallel",)),
        name="mm_small",
    )(a, w, b)


def _inproj_kernel(x_ref, g_ref, sh_ref, sc_ref, w_ref, o_ref, hx_ref):
    @pl.when(pl.program_id(1) == 0)
    def _():
        hx_ref[...] = _rms_mod(x_ref[...], g_ref[...], sh_ref[...], sc_ref[...]).astype(BF16)

    o_ref[...] = jnp.dot(hx_ref[...], w_ref[...], preferred_element_type=F32).astype(o_ref.dtype)


def _inproj(x, g, shift, scale, w, tm, tn):
    m, d = x.shape
    n = w.shape[1]
    row = lambda i, j: (0, 0)
    return pl.pallas_call(
        _inproj_kernel,
        grid=(m // tm, n // tn),
        in_specs=[pl.BlockSpec((tm, d), lambda i, j: (i, 0)),
                  pl.BlockSpec((1, d), row), pl.BlockSpec((1, d), row), pl.BlockSpec((1, d), row),
                  pl.BlockSpec((d, tn), lambda i, j: (0, j))],
        out_specs=pl.BlockSpec((tm, tn), lambda i, j: (i, j)),
        out_shape=jax.ShapeDtypeStruct((m, n), F32),
        scratch_shapes=[pltpu.VMEM((tm, d), BF16)],
        compiler_params=_cparams(("parallel", "arbitrary")),
        name="inproj",
    )(x, g, shift, scale, w)


def _ssd_kernel(*refs, reverse, final):
    if final:
        (xs_ref, b_ref, c_ref, dt_ref, dtb_ref, a_ref, exp_ref, sel_ref, h0_ref,
         yp_ref, z_ref, dsk_ref, nrm_ref, y_ref, hfin_ref, st_ref) = refs
    else:
        (xs_ref, b_ref, c_ref, dt_ref, dtb_ref, a_ref, exp_ref, sel_ref, h0_ref,
         y_ref, hfin_ref, st_ref) = refs
    c = pl.program_id(1)
    nc = pl.num_programs(1)

    @pl.when(c == 0)
    def _():
        st_ref[...] = h0_ref[...]

    rows = lax.broadcasted_iota(jnp.int32, (CHUNK, CHUNK), 0)
    cols = lax.broadcasted_iota(jnp.int32, (CHUNK, CHUNK), 1)
    tri = (rows <= cols) if reverse else (rows >= cols)
    tri_bf = tri.astype(BF16)
    lane_lo = cols < SSM_HEAD_DIM
    end_row = 0 if reverse else CHUNK - 1

    dt_all = _softplus(dt_ref[...] + dtb_ref[...])
    dta = dt_all * a_ref[...]
    acum = _dot01_l(tri_bf, dta)
    exp_m = exp_ref[...]
    acum_b = _dot01_r(acum, exp_m)
    dt_b = _dot01_r(dt_all, exp_m)
    acum_t = _dot_nt_split(sel_ref[...], acum)

    xs = xs_ref[...]
    bm = b_ref[...]
    cm = c_ref[...].astype(BF16)
    bm_t = bm.T.astype(BF16)
    scores = _dot_nt(cm, bm.astype(BF16))

    ys = []
    for j in range(SSM_HPG // 2):
        r0, r1 = 2 * j, 2 * j + 1
        ab0 = acum_b[:, LANES * r0:LANES * (r0 + 1)]
        ab1 = acum_b[:, LANES * r1:LANES * (r1 + 1)]
        acum_e = jnp.where(lane_lo, ab0, ab1)
        dt_e = jnp.where(lane_lo, dt_b[:, LANES * r0:LANES * (r0 + 1)], dt_b[:, LANES * r1:LANES * (r1 + 1)])
        xp = xs[:, LANES * j:LANES * (j + 1)] * dt_e
        end = acum_e[end_row:end_row + 1, :]
        xw = (xp * jnp.exp(end - acum_e)).astype(BF16)
        l0 = jnp.exp(jnp.where(tri, ab0 - acum_t[r0:r0 + 1, :], NEG_BIG))
        l1 = jnp.exp(jnp.where(tri, ab1 - acum_t[r1:r1 + 1, :], NEG_BIG))
        m0 = (scores * l0).astype(BF16)
        m1 = (scores * l1).astype(BF16)
        xp0 = jnp.where(lane_lo, xp, 0.0).astype(BF16)
        xp1 = jnp.where(lane_lo, 0.0, xp).astype(BF16)
        y_diag = (jnp.dot(m0, xp0, preferred_element_type=F32)
                  + jnp.dot(m1, xp1, preferred_element_type=F32))
        st = st_ref[j]
        y_off = jnp.dot(cm, st.astype(BF16), preferred_element_type=F32) * jnp.exp(acum_e)
        ys.append(y_diag + y_off)
        st_ref[j] = st * jnp.exp(end) + jnp.dot(bm_t, xw, preferred_element_type=F32)
    y = jnp.concatenate(ys, axis=1)

    if final:
        y = y + yp_ref[...] + xs * dsk_ref[...]
        y = y * _silu(z_ref[...])
        y = y * lax.rsqrt(jnp.mean(y * y, axis=-1, keepdims=True) + EPS)
        y = y * nrm_ref[...]
    y_ref[...] = y.astype(y_ref.dtype)

    @pl.when(c == nc - 1)
    def _():
        hfin_ref[...] = st_ref[...]


def _dot_nt_split(m01, x):
    hi, mid, lo = _split3(x)
    return _dot_nt(m01, hi) + _dot_nt(m01, mid) + _dot_nt(m01, lo)


def _ssd_tables(direction):
    exp_m = np.zeros((SSM_GROUPS, LANES, SSM_HPG * LANES), np.float32)
    sel = np.zeros((SSM_GROUPS, LANES, LANES), np.float32)
    for g in range(SSM_GROUPS):
        for r in range(SSM_HPG):
            col = direction * SSM_HEADS + g * SSM_HPG + r
            exp_m[g, col, LANES * r:LANES * (r + 1)] = 1.0
            sel[g, r, col] = 1.0
    return jnp.asarray(exp_m, BF16), jnp.asarray(sel, BF16)


def _ssd_pass(xbc, u_all, dtb_row, a_row, h0, direction, final_args=None):
    L = xbc.shape[0]
    nc = L // CHUNK
    reverse = direction == 1
    final = final_args is not None
    exp_m, sel = _ssd_tables(direction)
    gw = BRANCH_W // SSM_GROUPS
    cidx = (lambda c: nc - 1 - c) if reverse else (lambda c: c)
    in_specs = [
        pl.BlockSpec((CHUNK, gw), lambda g, c: (cidx(c), g)),
        pl.BlockSpec((CHUNK, SSM_STATE), lambda g, c: (cidx(c), BRANCH_W // SSM_STATE + g)),
        pl.BlockSpec((CHUNK, SSM_STATE), lambda g, c: (cidx(c), BRANCH_W // SSM_STATE + SSM_GROUPS + g)),
        pl.BlockSpec((CHUNK, LANES), lambda g, c: (cidx(c), OFF_DT // LANES)),
        pl.BlockSpec((1, LANES), lambda g, c: (0, 0)),
        pl.BlockSpec((1, LANES), lambda g, c: (0, 0)),
        pl.BlockSpec((None, LANES, SSM_HPG * LANES), lambda g, c: (g, 0, 0)),
        pl.BlockSpec((None, LANES, LANES), lambda g, c: (g, 0, 0)),
        pl.BlockSpec((None, SSM_HPG // 2, SSM_STATE, LANES), lambda g, c: (g, 0, 0, 0)),
    ]
    args = [xbc, xbc, xbc, u_all, dtb_row, a_row, exp_m, sel, h0]
    if final:
        y_prev, dsk_row, nrm_row = final_args
        in_specs += [
            pl.BlockSpec((CHUNK, gw), lambda g, c: (cidx(c), g)),
            pl.BlockSpec((CHUNK, gw), lambda g, c: (cidx(c), OFF_Z // gw + g)),
            pl.BlockSpec((1, gw), lambda g, c: (0, g)),
            pl.BlockSpec((1, gw), lambda g, c: (0, g)),
        ]
        args += [y_prev, u_all, dsk_row, nrm_row]
    out_dtype = BF16 if final else F32
    y, hfin = pl.pallas_call(
        functools.partial(_ssd_kernel, reverse=reverse, final=final),
        grid=(SSM_GROUPS, nc),
        in_specs=in_specs,
        out_specs=[pl.BlockSpec((CHUNK, gw), lambda g, c: (cidx(c), g)),
                   pl.BlockSpec((None, SSM_HPG // 2, SSM_STATE, LANES), lambda g, c: (g, 0, 0, 0))],
        out_shape=[jax.ShapeDtypeStruct((L, BRANCH_W), out_dtype),
                   jax.ShapeDtypeStruct((SSM_GROUPS, SSM_HPG // 2, SSM_STATE, LANES), F32)],
        scratch_shapes=[pltpu.VMEM((SSM_HPG // 2, SSM_STATE, LANES), F32)],
        compiler_params=_cparams(("parallel", "arbitrary")),
        name="ssd_bwd" if reverse else "ssd_fwd",
    )(*args)
    return y, hfin


def _conv_silu(u_all, conv_w, conv_b):
    L = u_all.shape[0]
    u = u_all[:, OFF_XBC:OFF_XBC + XBC_W]
    up = jnp.pad(u, ((CONV_W // 2, CONV_W // 2), (0, 0)))
    acc = conv_b[None, :]
    for k in range(CONV_W):
        acc = acc + up[k:k + L] * conv_w[k][None, :]
    return acc * jax.nn.sigmoid(acc)


def _pad_row(v, width=LANES):
    return jnp.zeros((1, width), F32).at[0, :v.shape[0]].set(v.astype(F32))


def _mamba(u_all, conv_w, conv_b, dt_bias, a_log, d_skip, ssm_norm, h0_f, h0_b):
    xbc = _conv_silu(u_all, conv_w, conv_b)
    dtb_row = _pad_row(dt_bias.reshape(-1))
    a_row = _pad_row(-jnp.exp(a_log.astype(F32)).reshape(-1))
    dsk_row = jnp.repeat(d_skip.astype(F32), SSM_HEAD_DIM)[None, :]
    nrm_row = ssm_norm.astype(F32)[None, :]
    y_f, hf = _ssd_pass(xbc, u_all, dtb_row, a_row, h0_f, 0)
    y, hb = _ssd_pass(xbc, u_all, dtb_row, a_row, h0_b, 1, (y_f, dsk_row, nrm_row))
    return y, hf, hb


def _gla_chunk(uq, uf, v, lb, st_t, ebc, bd_mask, reverse):
    Q = CHUNK
    rows = lax.broadcasted_iota(jnp.int32, (Q, LANES), 0)
    lanes = lax.broadcasted_iota(jnp.int32, (Q, LANES), 1)
    r2 = lax.broadcasted_iota(jnp.int32, (Q, Q), 0)
    c2 = lax.broadcasted_iota(jnp.int32, (Q, Q), 1)
    tri_bf = ((r2 <= c2) if reverse else (r2 >= c2)).astype(BF16)
    lane_lo = lanes < HGRN_KEY
    nsub = Q // SUB
    end_row = 0 if reverse else Q - 1

    q = _silu(uq)
    f = lb + (1.0 - lb) * _sigmoid(uf)
    logf = jnp.log(jnp.maximum(f, F_MIN))
    k = (1.0 - lb) * _sigmoid(-uf)
    gc = _dot01_l(tri_bf, logf)

    def ref_of(i):
        if reverse:
            return None if i == nsub - 1 else gc[SUB * (i + 1):SUB * (i + 1) + 1, :]
        return None if i == 0 else gc[SUB * i - 1:SUB * i, :]

    ref_rows = []
    for i in range(nsub):
        r = ref_of(i)
        ref_rows.append(jnp.zeros((SUB, LANES), F32) if r is None else jnp.broadcast_to(r, (SUB, LANES)))
    ref_full = jnp.concatenate(ref_rows, axis=0)
    qt = q * jnp.exp(gc - ref_full)
    qt0 = jnp.where(lane_lo, qt, 0.0).astype(BF16)
    qt1 = jnp.where(lane_lo, 0.0, qt).astype(BF16)

    a0_rows, a1_rows = [], []
    for i in range(nsub):
        r = ref_of(i)
        if r is None:
            a0_rows.append(jnp.zeros((SUB, Q), F32))
            a1_rows.append(jnp.zeros((SUB, Q), F32))
            continue
        seen = (rows >= SUB * (i + 1)) if reverse else (rows < SUB * i)
        kh = jnp.where(seen, k * jnp.exp(jnp.where(seen, r - gc, 0.0)), 0.0).astype(BF16)
        lhs = jnp.concatenate([qt0[SUB * i:SUB * (i + 1)], qt1[SUB * i:SUB * (i + 1)]], axis=0)
        att = _dot_nt(lhs, kh)
        a0_rows.append(att[:SUB])
        a1_rows.append(att[SUB:])
    a0 = jnp.concatenate(a0_rows, axis=0).astype(BF16)
    a1 = jnp.concatenate(a1_rows, axis=0).astype(BF16)
    vb = v.astype(BF16)
    o = jnp.concatenate([jnp.dot(a0, vb[:, :HGRN_VAL], preferred_element_type=F32),
                         jnp.dot(a1, vb[:, HGRN_VAL:], preferred_element_type=F32)], axis=1)

    pos = rows % SUB
    for d in range(SUB):
        if d == 0:
            p = q * k
            vs = v
        else:
            sh = (Q - d) if reverse else d
            ks = pltpu.roll(k, sh, 0)
            gs = pltpu.roll(gc, sh, 0)
            vs = pltpu.roll(v, sh, 0)
            ok = (pos + d < SUB) if reverse else (pos >= d)
            p = jnp.where(ok, q * ks * jnp.exp(jnp.where(ok, gc - gs, 0.0)), 0.0)
        o = o + jnp.dot(p.astype(BF16), ebc, preferred_element_type=F32) * vs

    qg = (q * jnp.exp(gc)).astype(BF16)
    o = o + _dot_nt(qg, st_t.astype(BF16))
    end = gc[end_row:end_row + 1, :]
    kd = (k * jnp.exp(end - gc)).astype(BF16)
    st_new = st_t * jnp.exp(end) + jnp.dot(v.T.astype(BF16), kd, preferred_element_type=F32) * bd_mask
    return o, st_new


def _gla_kernel(*refs, reverse, final, wb, nrow_chunks):
    if final:
        (q_ref, f_ref, v_ref, lb_ref, ebc_ref, s0_ref, op_ref, g_ref, nrm_ref, o_ref, sfin_ref, st_ref,
         bq, bf, bv, bo) = refs
    else:
        (q_ref, f_ref, v_ref, lb_ref, ebc_ref, s0_ref, o_ref, sfin_ref, st_ref, bq, bf, bv, bo) = refs
    c = pl.program_id(1)
    ncol = pl.num_programs(1)
    col = (ncol - 1 - c) if reverse else c
    jcol = col % wb

    @pl.when(c == 0)
    def _():
        st_ref[...] = s0_ref[...]

    for jj in range(wb):
        @pl.when(jcol == jj)
        def _():
            bq[...] = q_ref[:, jj, :]
            bf[...] = f_ref[:, jj, :]
            bv[...] = v_ref[:, jj, :]
            if final:
                bo[...] = op_ref[:, jj, :]

    r2 = lax.broadcasted_iota(jnp.int32, (2 * HGRN_VAL, LANES), 0)
    c2 = lax.broadcasted_iota(jnp.int32, (2 * HGRN_VAL, LANES), 1)
    bd_mask = ((r2 < HGRN_VAL) == (c2 < HGRN_KEY)).astype(F32)
    lb = lb_ref[...]
    ebc = ebc_ref[...]
    order = range(nrow_chunks - 1, -1, -1) if reverse else range(nrow_chunks)
    for rc in order:
        sl = pl.ds(rc * CHUNK, CHUNK)
        o, st_new = _gla_chunk(bq[sl, :], bf[sl, :], bv[sl, :], lb, st_ref[...], ebc, bd_mask, reverse)
        st_ref[...] = st_new
        if final:
            o = o + bo[sl, :]
        bo[sl, :] = o

    if final:
        nrm = nrm_ref[...]
        for jj in range(wb):
            @pl.when(jcol == jj)
            def _():
                ot = bo[...]
                gate = _silu(g_ref[:, jj, :])
                halves = []
                for hh in range(2):
                    oh = ot[:, HGRN_VAL * hh:HGRN_VAL * (hh + 1)]
                    oh = oh * lax.rsqrt(jnp.mean(oh * oh, axis=-1, keepdims=True) + EPS)
                    halves.append(oh)
                res = jnp.concatenate(halves, axis=1) * nrm * gate
                o_ref[:, jj, :] = res.astype(o_ref.dtype)
    else:
        for jj in range(wb):
            @pl.when(jcol == jj)
            def _():
                o_ref[:, jj, :] = bo[...]

    @pl.when(c == ncol - 1)
    def _():
        sfin_ref[...] = st_ref[...]


def _gla_pass(u3, lb_rows, s0, direction, wb, final_args=None):
    R, Wc, _ = u3.shape
    reverse = direction == 1
    final = final_args is not None
    cb = (lambda c: (Wc - 1 - c) // wb) if reverse else (lambda c: c // wb)
    ebc = np.zeros((LANES, 2 * HGRN_VAL), np.float32)
    ebc[:HGRN_KEY, :HGRN_VAL] = 1.0
    ebc[HGRN_KEY:, HGRN_VAL:] = 1.0
    ebc = jnp.asarray(ebc, BF16)
    vw = 2 * HGRN_VAL
    in_specs = [
        pl.BlockSpec((R, wb, LANES), lambda p, c: (0, cb(c), OFF_Q // LANES + p)),
        pl.BlockSpec((R, wb, LANES), lambda p, c: (0, cb(c), OFF_F // LANES + direction * HGRN_PAIRS + p)),
        pl.BlockSpec((R, wb, vw), lambda p, c: (0, cb(c), OFF_I // vw + p)),
        pl.BlockSpec((None, 1, LANES), lambda p, c: (p, 0, 0)),
        pl.BlockSpec((LANES, vw), lambda p, c: (0, 0)),
        pl.BlockSpec((None, vw, LANES), lambda p, c: (p, 0, 0)),
    ]
    args = [u3, u3, u3, lb_rows, ebc, s0]
    if final:
        o_prev, nrm_row = final_args
        in_specs += [
            pl.BlockSpec((R, wb, vw), lambda p, c: (0, cb(c), p)),
            pl.BlockSpec((R, wb, vw), lambda p, c: (0, cb(c), OFF_G // vw + p)),
            pl.BlockSpec((1, vw), lambda p, c: (0, 0)),
        ]
        args += [o_prev, u3, nrm_row]
    out_dtype = BF16 if final else F32
    o, sfin = pl.pallas_call(
        functools.partial(_gla_kernel, reverse=reverse, final=final, wb=wb, nrow_chunks=R // CHUNK),
        grid=(HGRN_PAIRS, Wc),
        in_specs=in_specs,
        out_specs=[pl.BlockSpec((R, wb, vw), lambda p, c: (0, cb(c), p)),
                   pl.BlockSpec((None, vw, LANES), lambda p, c: (p, 0, 0))],
        out_shape=[jax.ShapeDtypeStruct((R, Wc, BRANCH_W), out_dtype),
                   jax.ShapeDtypeStruct((HGRN_PAIRS, vw, LANES), F32)],
        scratch_shapes=[pltpu.VMEM((vw, LANES), F32),
                        pltpu.VMEM((R, LANES), F32), pltpu.VMEM((R, LANES), F32),
                        pltpu.VMEM((R, vw), F32), pltpu.VMEM((R, vw), F32)],
        compiler_params=_cparams(("parallel", "arbitrary")),
        name="gla_bwd" if reverse else "gla_fwd",
    )(*args)
    return o, sfin


def _hgrn(u_all, lower_bound, hgrn_norm, s0_f, s0_b, grid_w):
    L = u_all.shape[0]
    R = L // grid_w
    wb = min(grid_w, SUBLANES)
    u3 = u_all.reshape(R, grid_w, N_PAD)
    lb = lower_bound.astype(F32).reshape(2, HGRN_PAIRS, 1, LANES)
    nrm_row = jnp.tile(hgrn_norm.astype(F32), 2)[None, :]
    o_f, sf = _gla_pass(u3, lb[0], s0_f, 0, wb)
    y, sb = _gla_pass(u3, lb[1], s0_b, 1, wb, (o_f, nrm_row))
    return y.reshape(L, BRANCH_W), sf, sb


def _dft_tables(R, Wc):
    L = R * Wc
    q = np.arange(R)[None, :, None]
    r = np.arange(R)[None, None, :]
    c = np.arange(Wc)[:, None, None]
    ang = 2.0 * np.pi * ((q * (Wc * r + c)) % L) / L
    m = np.concatenate([np.cos(ang), -np.sin(ang)], axis=1)
    qc = np.arange(Wc)[:, None]
    cc = np.arange(Wc)[None, :]
    a2 = 2.0 * np.pi * ((qc * cc) % Wc) / Wc
    t = np.block([[np.cos(a2), np.sin(a2)], [np.sin(a2), -np.cos(a2)]])
    ch = np.arange(FOURIER_GW)
    a3 = 2.0 * np.pi * ((ch[:, None] * ch[None, :]) % FOURIER_GW) / FOURIER_GW
    cs = np.concatenate([np.cos(a3), -np.sin(a3)], axis=0) / math.sqrt(L * FOURIER_GW)
    return (jnp.asarray(m, F32).astype(BF16), jnp.asarray(t, F32).astype(BF16), jnp.asarray(cs, F32).astype(BF16))


def _fnet_a_kernel(x_ref, m_ref, y_ref, *, wb, R):
    for jj in range(wb):
        res = jnp.dot(m_ref[jj], x_ref[:, jj, :].astype(BF16), preferred_element_type=F32)
        y_ref[:, 0, jj, :] = res[:R]
        y_ref[:, 1, jj, :] = res[R:]


def _fnet_b_kernel(y_ref, t_ref, cs_ref, o_ref, *, Wc, rb):
    t = t_ref[...]
    cs = cs_ref[...]
    for i in range(rb):
        yy = y_ref[i].reshape(2 * Wc, FOURIER_GW).astype(BF16)
        pq = jnp.dot(t, yy, preferred_element_type=F32)
        lhs = jnp.concatenate([pq[:Wc], pq[Wc:]], axis=1).astype(BF16)
        o_ref[:, i, :] = jnp.dot(lhs, cs, preferred_element_type=F32).astype(o_ref.dtype)


def _fnet(u_all, grid_w):
    L = u_all.shape[0]
    R = L // grid_w
    Wc = grid_w
    wb = SUBLANES
    rb = SUBLANES
    m, t, cs = _dft_tables(R, Wc)
    u3 = u_all.reshape(R, Wc, N_PAD)
    y = pl.pallas_call(
        functools.partial(_fnet_a_kernel, wb=wb, R=R),
        grid=(Wc // wb, FOURIER_GROUPS),
        in_specs=[pl.BlockSpec((R, wb, FOURIER_GW), lambda cb, g: (0, cb, OFF_FOUR // FOURIER_GW + g)),
                  pl.BlockSpec((wb, 2 * R, R), lambda cb, g: (cb, 0, 0))],
        out_specs=pl.BlockSpec((R, 2, wb, FOURIER_GW), lambda cb, g: (0, 0, cb, g)),
        out_shape=jax.ShapeDtypeStruct((R, 2, Wc, BRANCH_W), F32),
        compiler_params=_cparams(("parallel", "arbitrary")),
        name="fnet_a",
    )(u3, m)
    out = pl.pallas_call(
        functools.partial(_fnet_b_kernel, Wc=Wc, rb=rb),
        grid=(R // rb, FOURIER_GROUPS),
        in_specs=[pl.BlockSpec((rb, 2, Wc, FOURIER_GW), lambda qb, g: (qb, 0, 0, g)),
                  pl.BlockSpec((2 * Wc, 2 * Wc), lambda qb, g: (0, 0)),
                  pl.BlockSpec((2 * FOURIER_GW, FOURIER_GW), lambda qb, g: (0, 0))],
        out_specs=pl.BlockSpec((Wc, rb, FOURIER_GW), lambda qb, g: (0, qb, g)),
        out_shape=jax.ShapeDtypeStruct((Wc, R, BRANCH_W), BF16),
        compiler_params=_cparams(("parallel", "arbitrary")),
        name="fnet_b",
    )(y, t, cs)
    return out.reshape(L, BRANCH_W)


def _fnet_small_kernel(x_ref, m_ref, cs_ref, o_ref, *, L):
    pq = jnp.dot(m_ref[...], x_ref[...].astype(BF16), preferred_element_type=F32)
    lhs = jnp.concatenate([pq[:L], -pq[L:]], axis=1).astype(BF16)
    o_ref[...] = jnp.dot(lhs, cs_ref[...], preferred_element_type=F32).astype(o_ref.dtype)


def _fnet_small(u_all):
    L = u_all.shape[0]
    m, _, cs = _dft_tables(L, 1)
    return pl.pallas_call(
        functools.partial(_fnet_small_kernel, L=L),
        grid=(FOURIER_GROUPS,),
        in_specs=[pl.BlockSpec((L, FOURIER_GW), lambda g: (0, OFF_FOUR // FOURIER_GW + g)),
                  pl.BlockSpec((2 * L, L), lambda g: (0, 0)),
                  pl.BlockSpec((2 * FOURIER_GW, FOURIER_GW), lambda g: (0, 0))],
        out_specs=pl.BlockSpec((L, FOURIER_GW), lambda g: (0, g)),
        out_shape=jax.ShapeDtypeStruct((L, BRANCH_W), BF16),
        compiler_params=_cparams(("parallel",)),
        name="fnet_small",
    )(u_all, m[0], cs)


def _merge_kernel(ym_ref, yh_ref, yf_ref, ug_ref, gu_ref, gb_ref, wbr_ref, wo_ref, x_ref, g1_ref, gt_ref,
                  o_ref, mg_ref, acc_ref, ss_ref, *, nb, tn):
    j = pl.program_id(1)

    @pl.when(j < nb)
    def _():
        ug = ug_ref[...].astype(BF16)
        tot = None
        for i, y_ref in enumerate((ym_ref, yh_ref, yf_ref)):
            gate = _sigmoid(jnp.dot(ug, gu_ref[i], preferred_element_type=F32) + gb_ref[i])
            term = gate * jnp.dot(y_ref[...], wbr_ref[i], preferred_element_type=F32)
            tot = term if tot is None else tot + term
        mg_ref[j] = tot.astype(BF16)

    @pl.when((j >= nb) & (j < 2 * nb))
    def _():
        acc = jnp.dot(mg_ref[0], wo_ref[0:tn, :], preferred_element_type=F32)
        for kk in range(1, nb):
            acc = acc + jnp.dot(mg_ref[kk], wo_ref[kk * tn:(kk + 1) * tn, :], preferred_element_type=F32)
        acc_ref[j - nb] = acc
        part = jnp.sum(acc * acc, axis=-1, keepdims=True)

        @pl.when(j == nb)
        def _():
            ss_ref[...] = part

        @pl.when(j > nb)
        def _():
            ss_ref[...] = ss_ref[...] + part

    @pl.when(j >= 2 * nb)
    def _():
        inv = lax.rsqrt(ss_ref[...] * (1.0 / (nb * tn)) + EPS)
        o_ref[...] = x_ref[...] + gt_ref[...] * ((acc_ref[j - 2 * nb] * inv) * g1_ref[...])


def _merge(ym, yh, yf, u_all, gate_up, gate_b, w_branch, w_out, x, g1, gate, tm, tn):
    m, d = x.shape
    nb = d // tn
    p1 = lambda i, j: jnp.minimum(j, nb - 1)
    p2 = lambda i, j: jnp.clip(j - nb, 0, nb - 1)
    p3 = lambda i, j: jnp.clip(j - 2 * nb, 0, nb - 1)
    ybs = pl.BlockSpec((tm, BRANCH_W), lambda i, j: (i, 0))
    return pl.pallas_call(
        functools.partial(_merge_kernel, nb=nb, tn=tn),
        grid=(m // tm, 3 * nb),
        in_specs=[ybs, ybs, ybs,
                  pl.BlockSpec((tm, GATE_RANK), lambda i, j: (i, OFF_GATE // GATE_RANK)),
                  pl.BlockSpec((3, GATE_RANK, tn), lambda i, j: (0, 0, p1(i, j))),
                  pl.BlockSpec((3, 1, tn), lambda i, j: (0, 0, p1(i, j))),
                  pl.BlockSpec((3, BRANCH_W, tn), lambda i, j: (0, 0, p1(i, j))),
                  pl.BlockSpec((d, tn), lambda i, j: (0, p2(i, j))),
                  pl.BlockSpec((tm, tn), lambda i, j: (i, p3(i, j))),
                  pl.BlockSpec((1, tn), lambda i, j: (0, p3(i, j))),
                  pl.BlockSpec((1, tn), lambda i, j: (0, p3(i, j)))],
        out_specs=pl.BlockSpec((tm, tn), lambda i, j: (i, p3(i, j))),
        out_shape=jax.ShapeDtypeStruct((m, d), F32),
        scratch_shapes=[pltpu.VMEM((nb, tm, tn), BF16), pltpu.VMEM((nb, tm, tn), F32), pltpu.VMEM((tm, 1), F32)],
        compiler_params=_cparams(("parallel", "arbitrary")),
        name="merge",
    )(ym, yh, yf, u_all, gate_up, gate_b, w_branch, w_out, x, g1, gate)


def _mlp_kernel(x_ref, g_ref, sh_ref, sc_ref, wu_ref, wd_ref, xb_ref, g3_ref, gt_ref, o_ref,
                hx_ref, acc_ref, inv_ref, *, nh, nb, tn):
    j = pl.program_id(1)

    @pl.when(j == 0)
    def _():
        hx_ref[...] = _rms_mod(x_ref[...], g_ref[...], sh_ref[...], sc_ref[...]).astype(BF16)

    @pl.when(j < nh)
    def _():
        h = jnp.dot(hx_ref[...], wu_ref[...], preferred_element_type=F32)
        h = jnp.square(jnp.maximum(h, 0.0)).astype(BF16)
        for kk in range(nb):
            part = jnp.dot(h, wd_ref[:, kk * tn:(kk + 1) * tn], preferred_element_type=F32)

            @pl.when(j == 0)
            def _():
                acc_ref[kk] = part

            @pl.when(j > 0)
            def _():
                acc_ref[kk] = acc_ref[kk] + part

    @pl.when(j == nh)
    def _():
        ss = jnp.sum(jnp.square(acc_ref[0]), axis=-1, keepdims=True)
        for kk in range(1, nb):
            ss = ss + jnp.sum(jnp.square(acc_ref[kk]), axis=-1, keepdims=True)
        inv_ref[...] = lax.rsqrt(ss * (1.0 / (nb * tn)) + EPS)

    @pl.when(j >= nh)
    def _():
        o_ref[...] = xb_ref[...] + gt_ref[...] * ((acc_ref[j - nh] * inv_ref[...]) * g3_ref[...])


def _mlp(x, g2, shift, scale, w_up, w_down, g3, gate, tm, th, tn):
    m, d = x.shape
    dff = w_up.shape[1]
    nh = dff // th
    nb = d // tn
    row = lambda i, j: (0, 0)
    ph = lambda i, j: jnp.minimum(j, nh - 1)
    pe = lambda i, j: jnp.clip(j - nh, 0, nb - 1)
    return pl.pallas_call(
        functools.partial(_mlp_kernel, nh=nh, nb=nb, tn=tn),
        grid=(m // tm, nh + nb),
        in_specs=[pl.BlockSpec((tm, d), lambda i, j: (i, 0), pipeline_mode=pl.Buffered(1)),
                  pl.BlockSpec((1, d), row), pl.BlockSpec((1, d), row), pl.BlockSpec((1, d), row),
                  pl.BlockSpec((d, th), lambda i, j: (0, ph(i, j))),
                  pl.BlockSpec((th, d), lambda i, j: (ph(i, j), 0)),
                  pl.BlockSpec((tm, tn), lambda i, j: (i, pe(i, j))),
                  pl.BlockSpec((1, tn), lambda i, j: (0, pe(i, j))),
                  pl.BlockSpec((1, tn), lambda i, j: (0, pe(i, j)))],
        out_specs=pl.BlockSpec((tm, tn), lambda i, j: (i, pe(i, j))),
        out_shape=jax.ShapeDtypeStruct((m, d), F32),
        scratch_shapes=[pltpu.VMEM((tm, d), BF16), pltpu.VMEM((nb, tm, tn), F32), pltpu.VMEM((tm, 1), F32)],
        compiler_params=_cparams(("parallel", "arbitrary")),
        name="mlp",
    )(x, g2, shift, scale, w_up, w_down, x, g3, gate)


def _permute_w_in(w):
    idx = np.cumsum(IN_SPLITS)[:-1].tolist()
    w_z, w_xbc, w_dt, w_q, w_f, w_i, w_g, w_four, w_gate = jnp.split(w, idx, axis=-1)
    pad = jnp.zeros((w.shape[0], N_PAD - sum(IN_SPLITS)), w.dtype)
    return jnp.concatenate([w_z, w_four, w_xbc, w_q, w_f, w_i, w_g, w_gate, w_dt, pad], axis=-1).astype(BF16)


def _state_zeros():
    hm0 = jnp.zeros((SSM_GROUPS, SSM_HPG // 2, SSM_STATE, LANES), F32)
    sh0 = jnp.zeros((HGRN_PAIRS, 2 * HGRN_VAL, LANES), F32)
    return hm0, sh0


def kernel(x, c, ctx, c_ctx, ada_down, ada_up, ada_b, norm_g, w_in, conv_w, conv_b, ssm_dt_bias, ssm_a_log,
           ssm_d, ssm_norm, hgrn_lb_logits, hgrn_norm, gate_up, gate_b, w_branch, w_out, mlp_up, mlp_down):
    assert x.shape[0] == 1 and ctx.shape[0] == 1
    xs = x[0]
    cs = ctx[0]
    L, d = xs.shape
    lc = cs.shape[0]
    tm_x, tm_c = 512, lc
    p_lb = jax.nn.softmax(hgrn_lb_logits.astype(F32), axis=0)
    lower_bounds = jnp.clip(jnp.cumsum(p_lb, axis=0) - p_lb[0], 0.0, 1.0)
    cond = jnp.zeros((SUBLANES, d), F32).at[0].set(c[0]).at[1].set(c_ctx)
    cond = cond * jax.nn.sigmoid(cond)
    hm0, sh0 = _state_zeros()
    zero_b = jnp.zeros((1, ada_down.shape[-1]), F32)
    for l in range(DEPTH):
        last = l == DEPTH - 1
        g = norm_g[l].astype(F32)
        mods = _mm_small(_mm_small(cond, ada_down[l], zero_b, ada_down.shape[-1]), ada_up[l], ada_b[l][None, :], 2048)
        sx1, cx1, gx1, sx2, cx2, gx2 = [mods[0:1, i * d:(i + 1) * d] for i in range(6)]
        sc1, cc1, gc1, sc2, cc2, gc2 = [mods[1:2, i * d:(i + 1) * d] for i in range(6)]
        w_in_l = _permute_w_in(w_in[l])
        gu_l, gb_l = gate_up[l].astype(BF16), gate_b[l].astype(F32)[:, None, :]
        wbr_l, wo_l = w_branch[l].astype(BF16), w_out[l].astype(BF16)
        wu_l, wd_l = mlp_up[l].astype(BF16), mlp_down[l].astype(BF16)
        mamba_p = (conv_w[l], conv_b[l], ssm_dt_bias[l], ssm_a_log[l], ssm_d[l], ssm_norm[l])

        uc = _inproj(cs, g[0:1], sc1, cc1, w_in_l, tm_c, 768)
        ym_c, hm_f, hm_b = _mamba(uc, *mamba_p, hm0, hm0)
        yh_c, sh_f, sh_b = _hgrn(uc, lower_bounds[l], hgrn_norm[l], sh0, sh0, 1)
        ux = _inproj(xs, g[0:1], sx1, cx1, w_in_l, tm_x, 768)
        ym, _, _ = _mamba(ux, *mamba_p, hm_f, hm_b)
        yh, _, _ = _hgrn(ux, lower_bounds[l], hgrn_norm[l], sh_f, sh_b, GRID_W)
        yf = _fnet(ux, GRID_W)
        xs = _merge(ym, yh, yf, ux, gu_l, gb_l, wbr_l, wo_l, xs, g[1:2], gx1, tm_x, 512)
        xs = _mlp(xs, g[2:3], sx2, cx2, wu_l, wd_l, g[3:4], gx2, tm_x, 512, 512)
        if not last:
            yf_c = _fnet_small(uc)
            cs = _merge(ym_c, yh_c, yf_c, uc, gu_l, gb_l, wbr_l, wo_l, cs, g[1:2], gc1, tm_c, 512)
            cs = _mlp(cs, g[2:3], sc2, cc2, wu_l, wd_l, g[3:4], gc2, tm_c, 512, 512)
    return xs[None]
```

```python
import functools
import math

import numpy as np
import jax
import jax.numpy as jnp
from jax import lax
from jax.experimental import pallas as pl
from jax.experimental.pallas import tpu as pltpu

F32 = jnp.float32
BF16 = jnp.bfloat16

D_MODEL = 4096
DEPTH = 4
GRID_W = 64
BRANCH_W = 1536
SSM_HEAD_DIM = 64
SSM_HEADS = 24
SSM_GROUPS = 4
SSM_HPG = 6
SSM_STATE = 128
CONV_W = 5
XBC_W = BRANCH_W + 2 * SSM_GROUPS * SSM_STATE
HGRN_VAL = 128
HGRN_HEADS = 12
HGRN_KEY = 64
HGRN_HK = HGRN_HEADS * HGRN_KEY
HGRN_PAIRS = HGRN_HEADS // 2
FOURIER_GROUPS = 4
FOURIER_GW = BRANCH_W // FOURIER_GROUPS
GATE_RANK = 256
D_FF = 4 * D_MODEL
CHUNK = 128
EPS = 1e-6
F_MIN = 1e-30
IN_SPLITS = (BRANCH_W, XBC_W, 2 * SSM_HEADS, HGRN_HK, 2 * HGRN_HK, BRANCH_W, BRANCH_W, BRANCH_W, GATE_RANK)

LANES = 128
SUBLANES = 8
VMEM_LIMIT = 56 * 1024 * 1024

OFF_Z = 0
OFF_FOUR = OFF_Z + BRANCH_W
OFF_XBC = OFF_FOUR + BRANCH_W
OFF_Q = OFF_XBC + XBC_W
OFF_F = OFF_Q + HGRN_HK
OFF_I = OFF_F + 2 * HGRN_HK
OFF_G = OFF_I + BRANCH_W
OFF_GATE = OFF_G + BRANCH_W
OFF_DT = OFF_GATE + GATE_RANK
N_PAD = 11520
SUB = 16
NEG_BIG = -1e30


def _cparams(sem):
    return pltpu.CompilerParams(dimension_semantics=sem, vmem_limit_bytes=VMEM_LIMIT)


def _sigmoid(x):
    return 1.0 / (1.0 + jnp.exp(-x))


def _silu(x):
    return x * _sigmoid(x)


def _softplus(x):
    return jnp.maximum(x, 0.0) + jnp.log1p(jnp.exp(-jnp.abs(x)))


def _split3(x):
    hi = x.astype(BF16)
    r1 = x - hi.astype(F32)
    mid = r1.astype(BF16)
    lo = (r1 - mid.astype(F32)).astype(BF16)
    return hi, mid, lo


def _dot01_r(x, m01):
    hi, mid, lo = _split3(x)
    d = lambda a: jnp.dot(a, m01, preferred_element_type=F32)
    return d(hi) + d(mid) + d(lo)


def _dot01_l(m01, x):
    hi, mid, lo = _split3(x)
    d = lambda a: jnp.dot(m01, a, preferred_element_type=F32)
    return d(hi) + d(mid) + d(lo)


def _dot_nt(a, b):
    return lax.dot_general(a, b, (((1,), (1,)), ((), ())), preferred_element_type=F32)


def _rms_mod(x, g, shift, scale):
    ms = jnp.mean(x * x, axis=-1, keepdims=True)
    y = (x * lax.rsqrt(ms + EPS)) * g
    return y * (1.0 + scale) + shift


def _mm_small_kernel(a_ref, w_ref, b_ref, o_ref):
    o_ref[...] = jnp.dot(a_ref[...].astype(BF16), w_ref[...].astype(BF16),
                         preferred_element_type=F32) + b_ref[...]


def _mm_small(a, w, b, tn):
    m, k = a.shape
    n = w.shape[1]
    return pl.pallas_call(
        _mm_small_kernel,
        grid=(n // tn,),
        in_specs=[pl.BlockSpec((m, k), lambda j: (0, 0)),
                  pl.BlockSpec((k, tn), lambda j: (0, j)),
                  pl.BlockSpec((1, tn), lambda j: (0, j))],
        out_specs=pl.BlockSpec((m, tn), lambda j: (0, j)),
        out_shape=jax.ShapeDtypeStruct((m, n), F32),
        compiler_params=_cparams(("parallel",)),
        name="mm_small",
    )(a, w, b)


def _inproj_kernel(x_ref, g_ref, sh_ref, sc_ref, w_ref, o_ref, hx_ref):
    @pl.when(pl.program_id(1) == 0)
    def _():
        hx_ref[...] = _rms_mod(x_ref[...], g_ref[...], sh_ref[...], sc_ref[...]).astype(BF16)

    o_ref[...] = jnp.dot(hx_ref[...], w_ref[...], preferred_element_type=F32).astype(o_ref.dtype)


def _inproj(x, g, shift, scale, w, tm, tn):
    m, d = x.shape
    n = w.shape[1]
    row = lambda i, j: (0, 0)
    return pl.pallas_call(
        _inproj_kernel,
        grid=(m // tm, n // tn),
        in_specs=[pl.BlockSpec((tm, d), lambda i, j: (i, 0)),
                  pl.BlockSpec((1, d), row), pl.BlockSpec((1, d), row), pl.BlockSpec((1, d), row),
                  pl.BlockSpec((d, tn), lambda i, j: (0, j))],
        out_specs=pl.BlockSpec((tm, tn), lambda i, j: (i, j)),
        out_shape=jax.ShapeDtypeStruct((m, n), F32),
        scratch_shapes=[pltpu.VMEM((tm, d), BF16)],
        compiler_params=_cparams(("parallel", "arbitrary")),
        name="inproj",
    )(x, g, shift, scale, w)


def _split2(x):
    hi = x.astype(BF16)
    return hi, (x - hi.astype(F32)).astype(BF16)


def _ssd_kernel(*refs, reverse, final, direction):
    if final:
        (xs_ref, b_ref, c_ref, dt_ref, dtb_ref, a_ref, exp_ref, h0_ref,
         yp_ref, z_ref, dsk_ref, nrm_ref, y_ref, hfin_ref, st_ref) = refs
    else:
        (xs_ref, b_ref, c_ref, dt_ref, dtb_ref, a_ref, exp_ref, h0_ref, y_ref, hfin_ref, st_ref) = refs
    c = pl.program_id(0)
    nc = pl.num_programs(0)
    gw = BRANCH_W // SSM_GROUPS

    @pl.when(c == 0)
    def _():
        st_ref[...] = h0_ref[...]

    rows = lax.broadcasted_iota(jnp.int32, (CHUNK, CHUNK), 0)
    cols = lax.broadcasted_iota(jnp.int32, (CHUNK, CHUNK), 1)
    tri = (rows <= cols) if reverse else (rows >= cols)
    tri_bf = tri.astype(BF16)
    lane_lo = cols < SSM_HEAD_DIM
    end_row = 0 if reverse else CHUNK - 1

    dt_all = _softplus(dt_ref[...] + dtb_ref[...])
    dta = dt_all * a_ref[...]
    acum = _dot01_l(tri_bf, dta)
    acum_t = acum.T
    a_hi, a_lo = _split2(acum)
    dt_bf = dt_all.astype(BF16)

    for g in range(SSM_GROUPS):
        exp_m = exp_ref[g]
        acum_b = (jnp.dot(a_hi, exp_m, preferred_element_type=F32)
                  + jnp.dot(a_lo, exp_m, preferred_element_type=F32))
        dt_b = jnp.dot(dt_bf, exp_m, preferred_element_type=F32)
        xs = xs_ref[:, gw * g:gw * (g + 1)]
        bm = b_ref[:, SSM_STATE * g:SSM_STATE * (g + 1)]
        cm = c_ref[:, SSM_STATE * g:SSM_STATE * (g + 1)].astype(BF16)
        bm_t = bm.T.astype(BF16)
        scores = _dot_nt(cm, bm.astype(BF16))
        ys = []
        for j in range(SSM_HPG // 2):
            r0, r1 = 2 * j, 2 * j + 1
            col0 = direction * SSM_HEADS + g * SSM_HPG + r0
            ab0 = acum_b[:, LANES * r0:LANES * (r0 + 1)]
            ab1 = acum_b[:, LANES * r1:LANES * (r1 + 1)]
            acum_e = jnp.where(lane_lo, ab0, ab1)
            dt_e = jnp.where(lane_lo, dt_b[:, LANES * r0:LANES * (r0 + 1)], dt_b[:, LANES * r1:LANES * (r1 + 1)])
            xp = xs[:, LANES * j:LANES * (j + 1)] * dt_e
            end = acum_e[end_row:end_row + 1, :]
            xw = (xp * jnp.exp(end - acum_e)).astype(BF16)
            l0 = jnp.exp(jnp.where(tri, ab0 - acum_t[col0:col0 + 1, :], NEG_BIG))
            l1 = jnp.exp(jnp.where(tri, ab1 - acum_t[col0 + 1:col0 + 2, :], NEG_BIG))
            m0 = (scores * l0).astype(BF16)
            m1 = (scores * l1).astype(BF16)
            xp0 = jnp.where(lane_lo, xp, 0.0).astype(BF16)
            xp1 = jnp.where(lane_lo, 0.0, xp).astype(BF16)
            y_diag = (jnp.dot(m0, xp0, preferred_element_type=F32)
                      + jnp.dot(m1, xp1, preferred_element_type=F32))
            st = st_ref[g, j]
            y_off = jnp.dot(cm, st.astype(BF16), preferred_element_type=F32) * jnp.exp(acum_e)
            ys.append(y_diag + y_off)
            st_ref[g, j] = st * jnp.exp(end) + jnp.dot(bm_t, xw, preferred_element_type=F32)
        y = jnp.concatenate(ys, axis=1)
        if final:
            gs = slice(gw * g, gw * (g + 1))
            y = y + yp_ref[:, gs] + xs * dsk_ref[:, gs]
            y = y * _silu(z_ref[:, gs])
            y = y * lax.rsqrt(jnp.mean(y * y, axis=-1, keepdims=True) + EPS)
            y = y * nrm_ref[:, gs]
        y_ref[:, gw * g:gw * (g + 1)] = y.astype(y_ref.dtype)

    @pl.when(c == nc - 1)
    def _():
        hfin_ref[...] = st_ref[...]


def _ssd_tables(direction):
    exp_m = np.zeros((SSM_GROUPS, LANES, SSM_HPG * LANES), np.float32)
    for g in range(SSM_GROUPS):
        for r in range(SSM_HPG):
            col = direction * SSM_HEADS + g * SSM_HPG + r
            exp_m[g, col, LANES * r:LANES * (r + 1)] = 1.0
    return jnp.asarray(exp_m, BF16)


def _ssd_pass(xbc, u_all, dtb_row, a_row, h0, direction, final_args=None):
    L = xbc.shape[0]
    nc = L // CHUNK
    reverse = direction == 1
    final = final_args is not None
    exp_m = _ssd_tables(direction)
    bc_w = SSM_GROUPS * SSM_STATE
    cidx = (lambda c: nc - 1 - c) if reverse else (lambda c: c)
    st_shape = (SSM_GROUPS, SSM_HPG // 2, SSM_STATE, LANES)
    in_specs = [
        pl.BlockSpec((CHUNK, BRANCH_W), lambda c: (cidx(c), 0)),
        pl.BlockSpec((CHUNK, bc_w), lambda c: (cidx(c), BRANCH_W // bc_w)),
        pl.BlockSpec((CHUNK, bc_w), lambda c: (cidx(c), BRANCH_W // bc_w + 1)),
        pl.BlockSpec((CHUNK, LANES), lambda c: (cidx(c), OFF_DT // LANES)),
        pl.BlockSpec((1, LANES), lambda c: (0, 0)),
        pl.BlockSpec((1, LANES), lambda c: (0, 0)),
        pl.BlockSpec((SSM_GROUPS, LANES, SSM_HPG * LANES), lambda c: (0, 0, 0)),
        pl.BlockSpec(st_shape, lambda c: (0, 0, 0, 0)),
    ]
    args = [xbc, xbc, xbc, u_all, dtb_row, a_row, exp_m, h0]
    if final:
        y_prev, dsk_row, nrm_row = final_args
        in_specs += [
            pl.BlockSpec((CHUNK, BRANCH_W), lambda c: (cidx(c), 0)),
            pl.BlockSpec((CHUNK, BRANCH_W), lambda c: (cidx(c), OFF_Z // BRANCH_W)),
            pl.BlockSpec((1, BRANCH_W), lambda c: (0, 0)),
            pl.BlockSpec((1, BRANCH_W), lambda c: (0, 0)),
        ]
        args += [y_prev, u_all, dsk_row, nrm_row]
    out_dtype = BF16 if final else F32
    y, hfin = pl.pallas_call(
        functools.partial(_ssd_kernel, reverse=reverse, final=final, direction=direction),
        grid=(nc,),
        in_specs=in_specs,
        out_specs=[pl.BlockSpec((CHUNK, BRANCH_W), lambda c: (cidx(c), 0)),
                   pl.BlockSpec(st_shape, lambda c: (0, 0, 0, 0))],
        out_shape=[jax.ShapeDtypeStruct((L, BRANCH_W), out_dtype),
                   jax.ShapeDtypeStruct(st_shape, F32)],
        scratch_shapes=[pltpu.VMEM(st_shape, F32)],
        compiler_params=_cparams(("arbitrary",)),
        name="ssd_bwd" if reverse else "ssd_fwd",
    )(*args)
    return y, hfin


def _conv_silu(u_all, conv_w, conv_b):
    L = u_all.shape[0]
    u = u_all[:, OFF_XBC:OFF_XBC + XBC_W]
    up = jnp.pad(u, ((CONV_W // 2, CONV_W // 2), (0, 0)))
    acc = conv_b[None, :]
    for k in range(CONV_W):
        acc = acc + up[k:k + L] * conv_w[k][None, :]
    return acc * jax.nn.sigmoid(acc)


def _pad_row(v, width=LANES):
    return jnp.zeros((1, width), F32).at[0, :v.shape[0]].set(v.astype(F32))


def _mamba(u_all, conv_w, conv_b, dt_bias, a_log, d_skip, ssm_norm, h0_f, h0_b):
    xbc = _conv_silu(u_all, conv_w, conv_b)
    dtb_row = _pad_row(dt_bias.reshape(-1))
    a_row = _pad_row(-jnp.exp(a_log.astype(F32)).reshape(-1))
    dsk_row = jnp.repeat(d_skip.astype(F32), SSM_HEAD_DIM)[None, :]
    nrm_row = ssm_norm.astype(F32)[None, :]
    y_f, hf = _ssd_pass(xbc, u_all, dtb_row, a_row, h0_f, 0)
    y, hb = _ssd_pass(xbc, u_all, dtb_row, a_row, h0_b, 1, (y_f, dsk_row, nrm_row))
    return y, hf, hb


def _gla_chunk(uq, uf, v, lb, st_t, ebc, sel_small, bd_mask, reverse):
    Q = CHUNK
    rows = lax.broadcasted_iota(jnp.int32, (Q, LANES), 0)
    lanes = lax.broadcasted_iota(jnp.int32, (Q, LANES), 1)
    r2 = lax.broadcasted_iota(jnp.int32, (Q, Q), 0)
    c2 = lax.broadcasted_iota(jnp.int32, (Q, Q), 1)
    tri_bf = ((r2 <= c2) if reverse else (r2 >= c2)).astype(BF16)
    lane_lo = lanes < HGRN_KEY
    end_row = 0 if reverse else Q - 1

    q = _silu(uq)
    f = lb + (1.0 - lb) * _sigmoid(uf)
    logf = jnp.log(jnp.maximum(f, F_MIN))
    k = (1.0 - lb) * _sigmoid(-uf)
    gc = _dot01_l(tri_bf, logf)

    xr = r2 ^ c2
    att = jnp.dot((q * k).astype(BF16), ebc, preferred_element_type=F32)
    a0 = jnp.where(xr == 0, att[:, :HGRN_VAL], 0.0)
    a1 = jnp.where(xr == 0, att[:, HGRN_VAL:], 0.0)
    b = 1
    lvl = 0
    while b < Q:
        if 2 * b < SUBLANES:
            ref = _dot01_l(sel_small[lvl], gc)
        else:
            pieces = []
            for p0 in range(0, Q, 2 * b):
                row = p0 + b if reverse else p0 + b - 1
                pieces.append(jnp.broadcast_to(gc[row:row + 1, :], (2 * b, LANES)))
            ref = jnp.concatenate(pieces, axis=0) if len(pieces) > 1 else pieces[0]
        hi = (rows & b) != 0
        is_q = (~hi) if reverse else hi
        qt = (q * jnp.exp(jnp.where(is_q, gc - ref, NEG_BIG))).astype(BF16)
        ek = ref - gc
        kt0 = (k * jnp.exp(jnp.where((~is_q) & lane_lo, ek, NEG_BIG))).astype(BF16)
        kt1 = (k * jnp.exp(jnp.where((~is_q) & (~lane_lo), ek, NEG_BIG))).astype(BF16)
        att = _dot_nt(qt, jnp.concatenate([kt0, kt1], axis=0))
        if 2 * b < Q:
            same = xr < 2 * b
            a0 = jnp.where(same & (xr >= b), att[:, :Q], a0)
            a1 = jnp.where(same & (xr >= b), att[:, Q:], a1)
        else:
            a0 = jnp.where(xr >= b, att[:, :Q], a0)
            a1 = jnp.where(xr >= b, att[:, Q:], a1)
        b *= 2
        lvl += 1
    vb = v.astype(BF16)
    o = jnp.concatenate([jnp.dot(a0.astype(BF16), vb[:, :HGRN_VAL], preferred_element_type=F32),
                         jnp.dot(a1.astype(BF16), vb[:, HGRN_VAL:], preferred_element_type=F32)], axis=1)

    qg = (q * jnp.exp(gc)).astype(BF16)
    o = o + _dot_nt(qg, st_t.astype(BF16))
    end = gc[end_row:end_row + 1, :]
    kd = (k * jnp.exp(end - gc)).astype(BF16)
    st_new = st_t * jnp.exp(end) + jnp.dot(v.T.astype(BF16), kd, preferred_element_type=F32) * bd_mask
    return o, st_new


def _gla_kernel(*refs, reverse, final, wb, nrow_chunks):
    if final:
        (q_ref, f_ref, v_ref, lb_ref, ebc_ref, sel_ref, s0_ref, op_ref, g_ref, nrm_ref, o_ref, sfin_ref, st_ref,
         bq, bf, bv, bo) = refs
    else:
        (q_ref, f_ref, v_ref, lb_ref, ebc_ref, sel_ref, s0_ref, o_ref, sfin_ref, st_ref, bq, bf, bv, bo) = refs
    c = pl.program_id(1)
    ncol = pl.num_programs(1)
    col = (ncol - 1 - c) if reverse else c
    jcol = col % wb

    @pl.when(c == 0)
    def _():
        st_ref[...] = s0_ref[...]

    for jj in range(wb):
        @pl.when(jcol == jj)
        def _():
            bq[...] = q_ref[:, jj, :]
            bf[...] = f_ref[:, jj, :]
            bv[...] = v_ref[:, jj, :]
            if final:
                bo[...] = op_ref[:, jj, :]

    r2 = lax.broadcasted_iota(jnp.int32, (2 * HGRN_VAL, LANES), 0)
    c2 = lax.broadcasted_iota(jnp.int32, (2 * HGRN_VAL, LANES), 1)
    bd_mask = ((r2 < HGRN_VAL) == (c2 < HGRN_KEY)).astype(F32)
    lb = lb_ref[...]
    ebc = ebc_ref[...]
    sel_small = [sel_ref[i] for i in range(sel_ref.shape[0])]
    order = range(nrow_chunks - 1, -1, -1) if reverse else range(nrow_chunks)
    for rc in order:
        sl = pl.ds(rc * CHUNK, CHUNK)
        o, st_new = _gla_chunk(bq[sl, :], bf[sl, :], bv[sl, :], lb, st_ref[...], ebc, sel_small, bd_mask, reverse)
        st_ref[...] = st_new
        if final:
            o = o + bo[sl, :]
        bo[sl, :] = o

    if final:
        nrm = nrm_ref[...]
        for jj in range(wb):
            @pl.when(jcol == jj)
            def _():
                ot = bo[...]
                gate = _silu(g_ref[:, jj, :])
                halves = []
                for hh in range(2):
                    oh = ot[:, HGRN_VAL * hh:HGRN_VAL * (hh + 1)]
                    oh = oh * lax.rsqrt(jnp.mean(oh * oh, axis=-1, keepdims=True) + EPS)
                    halves.append(oh)
                res = jnp.concatenate(halves, axis=1) * nrm * gate
                o_ref[:, jj, :] = res.astype(o_ref.dtype)
    else:
        for jj in range(wb):
            @pl.when(jcol == jj)
            def _():
                o_ref[:, jj, :] = bo[...]

    @pl.when(c == ncol - 1)
    def _():
        sfin_ref[...] = st_ref[...]


def _gla_pass(u3, lb_rows, s0, direction, wb, final_args=None):
    R, Wc, _ = u3.shape
    reverse = direction == 1
    final = final_args is not None
    cb = (lambda c: (Wc - 1 - c) // wb) if reverse else (lambda c: c // wb)
    ebc = np.zeros((LANES, 2 * HGRN_VAL), np.float32)
    ebc[:HGRN_KEY, :HGRN_VAL] = 1.0
    ebc[HGRN_KEY:, HGRN_VAL:] = 1.0
    ebc = jnp.asarray(ebc, BF16)
    small = [b for b in (1, 2, 4) if 2 * b < SUBLANES]
    sel = np.zeros((len(small), CHUNK, CHUNK), np.float32)
    for i, b in enumerate(small):
        for r in range(CHUNK):
            p0 = (r // (2 * b)) * 2 * b
            sel[i, r, p0 + b if reverse else p0 + b - 1] = 1.0
    sel = jnp.asarray(sel, BF16)
    vw = 2 * HGRN_VAL
    in_specs = [
        pl.BlockSpec((R, wb, LANES), lambda p, c: (0, cb(c), OFF_Q // LANES + p)),
        pl.BlockSpec((R, wb, LANES), lambda p, c: (0, cb(c), OFF_F // LANES + direction * HGRN_PAIRS + p)),
        pl.BlockSpec((R, wb, vw), lambda p, c: (0, cb(c), OFF_I // vw + p)),
        pl.BlockSpec((None, 1, LANES), lambda p, c: (p, 0, 0)),
        pl.BlockSpec((LANES, vw), lambda p, c: (0, 0)),
        pl.BlockSpec((len(small), CHUNK, CHUNK), lambda p, c: (0, 0, 0)),
        pl.BlockSpec((None, vw, LANES), lambda p, c: (p, 0, 0)),
    ]
    args = [u3, u3, u3, lb_rows, ebc, sel, s0]
    if final:
        o_prev, nrm_row = final_args
        in_specs += [
            pl.BlockSpec((R, wb, vw), lambda p, c: (0, cb(c), p)),
            pl.BlockSpec((R, wb, vw), lambda p, c: (0, cb(c), OFF_G // vw + p)),
            pl.BlockSpec((1, vw), lambda p, c: (0, 0)),
        ]
        args += [o_prev, u3, nrm_row]
    out_dtype = BF16 if final else F32
    o, sfin = pl.pallas_call(
        functools.partial(_gla_kernel, reverse=reverse, final=final, wb=wb, nrow_chunks=R // CHUNK),
        grid=(HGRN_PAIRS, Wc),
        in_specs=in_specs,
        out_specs=[pl.BlockSpec((R, wb, vw), lambda p, c: (0, cb(c), p)),
                   pl.BlockSpec((None, vw, LANES), lambda p, c: (p, 0, 0))],
        out_shape=[jax.ShapeDtypeStruct((R, Wc, BRANCH_W), out_dtype),
                   jax.ShapeDtypeStruct((HGRN_PAIRS, vw, LANES), F32)],
        scratch_shapes=[pltpu.VMEM((vw, LANES), F32),
                        pltpu.VMEM((R, LANES), F32), pltpu.VMEM((R, LANES), F32),
                        pltpu.VMEM((R, vw), F32), pltpu.VMEM((R, vw), F32)],
        compiler_params=_cparams(("parallel", "arbitrary")),
        name="gla_bwd" if reverse else "gla_fwd",
    )(*args)
    return o, sfin


def _hgrn(u_all, lower_bound, hgrn_norm, s0_f, s0_b, grid_w):
    L = u_all.shape[0]
    R = L // grid_w
    wb = min(grid_w, SUBLANES)
    u3 = u_all.reshape(R, grid_w, N_PAD)
    lb = lower_bound.astype(F32).reshape(2, HGRN_PAIRS, 1, LANES)
    nrm_row = jnp.tile(hgrn_norm.astype(F32), 2)[None, :]
    o_f, sf = _gla_pass(u3, lb[0], s0_f, 0, wb)
    y, sb = _gla_pass(u3, lb[1], s0_b, 1, wb, (o_f, nrm_row))
    return y.reshape(L, BRANCH_W), sf, sb


def _dft_tables(R, Wc):
    L = R * Wc
    q = np.arange(R)[None, :, None]
    r = np.arange(R)[None, None, :]
    c = np.arange(Wc)[:, None, None]
    ang = 2.0 * np.pi * ((q * (Wc * r + c)) % L) / L
    m = np.concatenate([np.cos(ang), -np.sin(ang)], axis=1)
    qc = np.arange(Wc)[:, None]
    cc = np.arange(Wc)[None, :]
    a2 = 2.0 * np.pi * ((qc * cc) % Wc) / Wc
    t = np.block([[np.cos(a2), np.sin(a2)], [np.sin(a2), -np.cos(a2)]])
    ch = np.arange(FOURIER_GW)
    a3 = 2.0 * np.pi * ((ch[:, None] * ch[None, :]) % FOURIER_GW) / FOURIER_GW
    cs = np.concatenate([np.cos(a3), -np.sin(a3)], axis=0) / math.sqrt(L * FOURIER_GW)
    return (jnp.asarray(m, F32).astype(BF16), jnp.asarray(t, F32).astype(BF16), jnp.asarray(cs, F32).astype(BF16))


def _fnet_a_kernel(x_ref, m_ref, y_ref, *, wb, R):
    for jj in range(wb):
        res = jnp.dot(m_ref[jj], x_ref[:, jj, :].astype(BF16), preferred_element_type=F32)
        y_ref[:, 0, jj, :] = res[:R]
        y_ref[:, 1, jj, :] = res[R:]


def _fnet_b_kernel(y_ref, t_ref, cs_ref, o_ref, *, Wc, rb):
    t = t_ref[...]
    cs = cs_ref[...]
    for i in range(rb):
        yy = y_ref[i].reshape(2 * Wc, FOURIER_GW).astype(BF16)
        pq = jnp.dot(t, yy, preferred_element_type=F32)
        lhs = jnp.concatenate([pq[:Wc], pq[Wc:]], axis=1).astype(BF16)
        o_ref[:, i, :] = jnp.dot(lhs, cs, preferred_element_type=F32).astype(o_ref.dtype)


def _fnet(u_all, grid_w):
    L = u_all.shape[0]
    R = L // grid_w
    Wc = grid_w
    wb = SUBLANES
    rb = SUBLANES
    m, t, cs = _dft_tables(R, Wc)
    u3 = u_all.reshape(R, Wc, N_PAD)
    y = pl.pallas_call(
        functools.partial(_fnet_a_kernel, wb=wb, R=R),
        grid=(Wc // wb, FOURIER_GROUPS),
        in_specs=[pl.BlockSpec((R, wb, FOURIER_GW), lambda cb, g: (0, cb, OFF_FOUR // FOURIER_GW + g)),
                  pl.BlockSpec((wb, 2 * R, R), lambda cb, g: (cb, 0, 0))],
        out_specs=pl.BlockSpec((R, 2, wb, FOURIER_GW), lambda cb, g: (0, 0, cb, g)),
        out_shape=jax.ShapeDtypeStruct((R, 2, Wc, BRANCH_W), F32),
        compiler_params=_cparams(("parallel", "arbitrary")),
        name="fnet_a",
    )(u3, m)
    out = pl.pallas_call(
        functools.partial(_fnet_b_kernel, Wc=Wc, rb=rb),
        grid=(R // rb, FOURIER_GROUPS),
        in_specs=[pl.BlockSpec((rb, 2, Wc, FOURIER_GW), lambda qb, g: (qb, 0, 0, g)),
                  pl.BlockSpec((2 * Wc, 2 * Wc), lambda qb, g: (0, 0)),
                  pl.BlockSpec((2 * FOURIER_GW, FOURIER_GW), lambda qb, g: (0, 0))],
        out_specs=pl.BlockSpec((Wc, rb, FOURIER_GW), lambda qb, g: (0, qb, g)),
        out_shape=jax.ShapeDtypeStruct((Wc, R, BRANCH_W), BF16),
        compiler_params=_cparams(("parallel", "arbitrary")),
        name="fnet_b",
    )(y, t, cs)
    return out.reshape(L, BRANCH_W)


def _fnet_small_kernel(x_ref, m_ref, cs_ref, o_ref, *, L):
    pq = jnp.dot(m_ref[...], x_ref[...].astype(BF16), preferred_element_type=F32)
    lhs = jnp.concatenate([pq[:L], -pq[L:]], axis=1).astype(BF16)
    o_ref[...] = jnp.dot(lhs, cs_ref[...], preferred_element_type=F32).astype(o_ref.dtype)


def _fnet_small(u_all):
    L = u_all.shape[0]
    m, _, cs = _dft_tables(L, 1)
    return pl.pallas_call(
        functools.partial(_fnet_small_kernel, L=L),
        grid=(FOURIER_GROUPS,),
        in_specs=[pl.BlockSpec((L, FOURIER_GW), lambda g: (0, OFF_FOUR // FOURIER_GW + g)),
                  pl.BlockSpec((2 * L, L), lambda g: (0, 0)),
                  pl.BlockSpec((2 * FOURIER_GW, FOURIER_GW), lambda g: (0, 0))],
        out_specs=pl.BlockSpec((L, FOURIER_GW), lambda g: (0, g)),
        out_shape=jax.ShapeDtypeStruct((L, BRANCH_W), BF16),
        compiler_params=_cparams(("parallel",)),
        name="fnet_small",
    )(u_all, m[0], cs)


def _merge_kernel(ym_ref, yh_ref, yf_ref, ug_ref, gu_ref, gb_ref, wbr_ref, wo_ref, x_ref, g1_ref, gt_ref,
                  o_ref, mg_ref, acc_ref, ss_ref, *, nb, tn):
    j = pl.program_id(1)

    @pl.when(j < nb)
    def _():
        ug = ug_ref[...].astype(BF16)
        tot = None
        for i, y_ref in enumerate((ym_ref, yh_ref, yf_ref)):
            gate = _sigmoid(jnp.dot(ug, gu_ref[i], preferred_element_type=F32) + gb_ref[i])
            term = gate * jnp.dot(y_ref[...], wbr_ref[i], preferred_element_type=F32)
            tot = term if tot is None else tot + term
        mg_ref[j] = tot.astype(BF16)

    @pl.when((j >= nb) & (j < 2 * nb))
    def _():
        acc = jnp.dot(mg_ref[0], wo_ref[0:tn, :], preferred_element_type=F32)
        for kk in range(1, nb):
            acc = acc + jnp.dot(mg_ref[kk], wo_ref[kk * tn:(kk + 1) * tn, :], preferred_element_type=F32)
        acc_ref[j - nb] = acc
        part = jnp.sum(acc * acc, axis=-1, keepdims=True)

        @pl.when(j == nb)
        def _():
            ss_ref[...] = part

        @pl.when(j > nb)
        def _():
            ss_ref[...] = ss_ref[...] + part

    @pl.when(j >= 2 * nb)
    def _():
        inv = lax.rsqrt(ss_ref[...] * (1.0 / (nb * tn)) + EPS)
        o_ref[...] = x_ref[...] + gt_ref[...] * ((acc_ref[j - 2 * nb] * inv) * g1_ref[...])


def _merge(ym, yh, yf, u_all, gate_up, gate_b, w_branch, w_out, x, g1, gate, tm, tn):
    m, d = x.shape
    nb = d // tn
    p1 = lambda i, j: jnp.minimum(j, nb - 1)
    p2 = lambda i, j: jnp.clip(j - nb, 0, nb - 1)
    p3 = lambda i, j: jnp.clip(j - 2 * nb, 0, nb - 1)
    ybs = pl.BlockSpec((tm, BRANCH_W), lambda i, j: (i, 0))
    return pl.pallas_call(
        functools.partial(_merge_kernel, nb=nb, tn=tn),
        grid=(m // tm, 3 * nb),
        in_specs=[ybs, ybs, ybs,
                  pl.BlockSpec((tm, GATE_RANK), lambda i, j: (i, OFF_GATE // GATE_RANK)),
                  pl.BlockSpec((3, GATE_RANK, tn), lambda i, j: (0, 0, p1(i, j))),
                  pl.BlockSpec((3, 1, tn), lambda i, j: (0, 0, p1(i, j))),
                  pl.BlockSpec((3, BRANCH_W, tn), lambda i, j: (0, 0, p1(i, j))),
                  pl.BlockSpec((d, tn), lambda i, j: (0, p2(i, j))),
                  pl.BlockSpec((tm, tn), lambda i, j: (i, p3(i, j))),
                  pl.BlockSpec((1, tn), lambda i, j: (0, p3(i, j))),
                  pl.BlockSpec((1, tn), lambda i, j: (0, p3(i, j)))],
        out_specs=pl.BlockSpec((tm, tn), lambda i, j: (i, p3(i, j))),
        out_shape=jax.ShapeDtypeStruct((m, d), F32),
        scratch_shapes=[pltpu.VMEM((nb, tm, tn), BF16), pltpu.VMEM((nb, tm, tn), F32), pltpu.VMEM((tm, 1), F32)],
        compiler_params=_cparams(("parallel", "arbitrary")),
        name="merge",
    )(ym, yh, yf, u_all, gate_up, gate_b, w_branch, w_out, x, g1, gate)


def _mlp_kernel(x_ref, g_ref, sh_ref, sc_ref, wu_ref, wd_ref, xb_ref, g3_ref, gt_ref, o_ref,
                hx_ref, h0_ref, h1_ref, acc_ref, inv_ref, *, nh, nb, tn):
    j = pl.program_id(1)

    def up(h_ref):
        h = jnp.dot(hx_ref[...], wu_ref[...], preferred_element_type=F32)
        h_ref[...] = jnp.square(jnp.maximum(h, 0.0)).astype(BF16)

    def down(h_ref):
        h = h_ref[...]
        for kk in range(nb):
            acc_ref[kk] = acc_ref[kk] + jnp.dot(h, wd_ref[:, kk * tn:(kk + 1) * tn], preferred_element_type=F32)

    @pl.when(j == 0)
    def _():
        hx_ref[...] = _rms_mod(x_ref[...], g_ref[...], sh_ref[...], sc_ref[...]).astype(BF16)
        acc_ref[...] = jnp.zeros(acc_ref.shape, F32)
        up(h0_ref)

    @pl.when((j > 0) & (j < nh) & (j % 2 == 1))
    def _():
        down(h0_ref)
        up(h1_ref)

    @pl.when((j > 0) & (j < nh) & (j % 2 == 0))
    def _():
        down(h1_ref)
        up(h0_ref)

    @pl.when(j == nh)
    def _():
        down(h1_ref if (nh - 1) % 2 == 1 else h0_ref)
        ss = jnp.sum(jnp.square(acc_ref[0]), axis=-1, keepdims=True)
        for kk in range(1, nb):
            ss = ss + jnp.sum(jnp.square(acc_ref[kk]), axis=-1, keepdims=True)
        inv_ref[...] = lax.rsqrt(ss * (1.0 / (nb * tn)) + EPS)

    @pl.when(j > nh)
    def _():
        o_ref[...] = xb_ref[...] + gt_ref[...] * ((acc_ref[j - nh - 1] * inv_ref[...]) * g3_ref[...])


def _mlp(x, g2, shift, scale, w_up, w_down, g3, gate, tm, th, tn):
    m, d = x.shape
    dff = w_up.shape[1]
    nh = dff // th
    nb = d // tn
    row = lambda i, j: (0, 0)
    ph = lambda i, j: jnp.minimum(j, nh - 1)
    pd = lambda i, j: jnp.clip(j - 1, 0, nh - 1)
    pe = lambda i, j: jnp.clip(j - nh - 1, 0, nb - 1)
    return pl.pallas_call(
        functools.partial(_mlp_kernel, nh=nh, nb=nb, tn=tn),
        grid=(m // tm, nh + 1 + nb),
        in_specs=[pl.BlockSpec((tm, d), lambda i, j: (i, 0), pipeline_mode=pl.Buffered(1)),
                  pl.BlockSpec((1, d), row), pl.BlockSpec((1, d), row), pl.BlockSpec((1, d), row),
                  pl.BlockSpec((d, th), lambda i, j: (0, ph(i, j))),
                  pl.BlockSpec((th, d), lambda i, j: (pd(i, j), 0)),
                  pl.BlockSpec((tm, tn), lambda i, j: (i, pe(i, j))),
                  pl.BlockSpec((1, tn), lambda i, j: (0, pe(i, j))),
                  pl.BlockSpec((1, tn), lambda i, j: (0, pe(i, j)))],
        out_specs=pl.BlockSpec((tm, tn), lambda i, j: (i, pe(i, j))),
        out_shape=jax.ShapeDtypeStruct((m, d), F32),
        scratch_shapes=[pltpu.VMEM((tm, d), BF16), pltpu.VMEM((tm, th), BF16), pltpu.VMEM((tm, th), BF16),
                        pltpu.VMEM((nb, tm, tn), F32), pltpu.VMEM((tm, 1), F32)],
        compiler_params=_cparams(("parallel", "arbitrary")),
        name="mlp",
    )(x, g2, shift, scale, w_up, w_down, x, g3, gate)


def _permute_w_in(w):
    idx = np.cumsum(IN_SPLITS)[:-1].tolist()
    w_z, w_xbc, w_dt, w_q, w_f, w_i, w_g, w_four, w_gate = jnp.split(w, idx, axis=-1)
    pad = jnp.zeros((w.shape[0], N_PAD - sum(IN_SPLITS)), w.dtype)
    return jnp.concatenate([w_z, w_four, w_xbc, w_q, w_f, w_i, w_g, w_gate, w_dt, pad], axis=-1).astype(BF16)


def _state_zeros():
    hm0 = jnp.zeros((SSM_GROUPS, SSM_HPG // 2, SSM_STATE, LANES), F32)
    sh0 = jnp.zeros((HGRN_PAIRS, 2 * HGRN_VAL, LANES), F32)
    return hm0, sh0


def kernel(x, c, ctx, c_ctx, ada_down, ada_up, ada_b, norm_g, w_in, conv_w, conv_b, ssm_dt_bias, ssm_a_log,
           ssm_d, ssm_norm, hgrn_lb_logits, hgrn_norm, gate_up, gate_b, w_branch, w_out, mlp_up, mlp_down):
    assert x.shape[0] == 1 and ctx.shape[0] == 1
    xs = x[0]
    cs = ctx[0]
    L, d = xs.shape
    lc = cs.shape[0]
    tm_x, tm_c = 512, lc
    p_lb = jax.nn.softmax(hgrn_lb_logits.astype(F32), axis=0)
    lower_bounds = jnp.clip(jnp.cumsum(p_lb, axis=0) - p_lb[0], 0.0, 1.0)
    cond = jnp.zeros((SUBLANES, d), F32).at[0].set(c[0]).at[1].set(c_ctx)
    cond = cond * jax.nn.sigmoid(cond)
    hm0, sh0 = _state_zeros()
    zero_b = jnp.zeros((1, ada_down.shape[-1]), F32)
    for l in range(DEPTH):
        last = l == DEPTH - 1
        g = norm_g[l].astype(F32)
        mods = _mm_small(_mm_small(cond, ada_down[l], zero_b, ada_down.shape[-1]), ada_up[l], ada_b[l][None, :], 2048)
        sx1, cx1, gx1, sx2, cx2, gx2 = [mods[0:1, i * d:(i + 1) * d] for i in range(6)]
        sc1, cc1, gc1, sc2, cc2, gc2 = [mods[1:2, i * d:(i + 1) * d] for i in range(6)]
        w_in_l = _permute_w_in(w_in[l])
        gu_l, gb_l = gate_up[l].astype(BF16), gate_b[l].astype(F32)[:, None, :]
        wbr_l, wo_l = w_branch[l].astype(BF16), w_out[l].astype(BF16)
        wu_l, wd_l = mlp_up[l].astype(BF16), mlp_down[l].astype(BF16)
        mamba_p = (conv_w[l], conv_b[l], ssm_dt_bias[l], ssm_a_log[l], ssm_d[l], ssm_norm[l])

        uc = _inproj(cs, g[0:1], sc1, cc1, w_in_l, tm_c, 768)
        ym_c, hm_f, hm_b = _mamba(uc, *mamba_p, hm0, hm0)
        yh_c, sh_f, sh_b = _hgrn(uc, lower_bounds[l], hgrn_norm[l], sh0, sh0, 1)
        ux = _inproj(xs, g[0:1], sx1, cx1, w_in_l, tm_x, 768)
        ym, _, _ = _mamba(ux, *mamba_p, hm_f, hm_b)
        yh, _, _ = _hgrn(ux, lower_bounds[l], hgrn_norm[l], sh_f, sh_b, GRID_W)
        yf = _fnet(ux, GRID_W)
        xs = _merge(ym, yh, yf, ux, gu_l, gb_l, wbr_l, wo_l, xs, g[1:2], gx1, tm_x, 512)
        xs = _mlp(xs, g[2:3], sx2, cx2, wu_l, wd_l, g[3:4], gx2, tm_x, 512, 512)
        if not last:
            yf_c = _fnet_small(uc)
            cs = _merge(ym_c, yh_c, yf_c, uc, gu_l, gb_l, wbr_l, wo_l, cs, g[1:2], gc1, tm_c, 512)
            cs = _mlp(cs, g[2:3], sc2, cc2, wu_l, wd_l, g[3:4], gc2, tm_c, 512, 512)
    return xs[None]
```

```python
import functools
import math

import numpy as np
import jax
import jax.numpy as jnp
from jax import lax
from jax.experimental import pallas as pl
from jax.experimental.pallas import tpu as pltpu

F32 = jnp.float32
BF16 = jnp.bfloat16

D_MODEL = 4096
DEPTH = 4
GRID_W = 64
BRANCH_W = 1536
SSM_HEAD_DIM = 64
SSM_HEADS = 24
SSM_GROUPS = 4
SSM_HPG = 6
SSM_STATE = 128
CONV_W = 5
XBC_W = BRANCH_W + 2 * SSM_GROUPS * SSM_STATE
HGRN_VAL = 128
HGRN_HEADS = 12
HGRN_KEY = 64
HGRN_HK = HGRN_HEADS * HGRN_KEY
HGRN_PAIRS = HGRN_HEADS // 2
FOURIER_GROUPS = 4
FOURIER_GW = BRANCH_W // FOURIER_GROUPS
GATE_RANK = 256
D_FF = 4 * D_MODEL
CHUNK = 128
EPS = 1e-6
F_MIN = 1e-30
IN_SPLITS = (BRANCH_W, XBC_W, 2 * SSM_HEADS, HGRN_HK, 2 * HGRN_HK, BRANCH_W, BRANCH_W, BRANCH_W, GATE_RANK)

LANES = 128
SUBLANES = 8
VMEM_LIMIT = 56 * 1024 * 1024

OFF_Z = 0
OFF_FOUR = OFF_Z + BRANCH_W
OFF_XBC = OFF_FOUR + BRANCH_W
OFF_Q = OFF_XBC + XBC_W
OFF_F = OFF_Q + HGRN_HK
OFF_I = OFF_F + 2 * HGRN_HK
OFF_G = OFF_I + BRANCH_W
OFF_GATE = OFF_G + BRANCH_W
OFF_DT = OFF_GATE + GATE_RANK
N_PAD = 11520
NEG_BIG = -1e30


def _cparams(sem):
    return pltpu.CompilerParams(dimension_semantics=sem, vmem_limit_bytes=VMEM_LIMIT)


def _sigmoid(x):
    return 1.0 / (1.0 + jnp.exp(-x))


def _silu(x):
    return x * _sigmoid(x)


def _softplus(x):
    return jnp.maximum(x, 0.0) + jnp.log1p(jnp.exp(-jnp.abs(x)))


def _split3(x):
    hi = x.astype(BF16)
    r1 = x - hi.astype(F32)
    mid = r1.astype(BF16)
    lo = (r1 - mid.astype(F32)).astype(BF16)
    return hi, mid, lo


def _split2(x):
    hi = x.astype(BF16)
    return hi, (x - hi.astype(F32)).astype(BF16)


def _dot01_l(m01, x):
    hi, mid, lo = _split3(x)
    d = lambda a: jnp.dot(m01, a, preferred_element_type=F32)
    return d(hi) + d(mid) + d(lo)


def _dot_nt(a, b):
    return lax.dot_general(a, b, (((1,), (1,)), ((), ())), preferred_element_type=F32)


def _rms_mod(x, g, shift, scale):
    ms = jnp.mean(x * x, axis=-1, keepdims=True)
    y = (x * lax.rsqrt(ms + EPS)) * g
    return y * (1.0 + scale) + shift


def _mm_small_kernel(a_ref, w_ref, b_ref, o_ref):
    o_ref[...] = jnp.dot(a_ref[...].astype(BF16), w_ref[...].astype(BF16),
                         preferred_element_type=F32) + b_ref[...]


def _mm_small(a, w, b, tn):
    m, k = a.shape
    n = w.shape[1]
    return pl.pallas_call(
        _mm_small_kernel,
        grid=(n // tn,),
        in_specs=[pl.BlockSpec((m, k), lambda j: (0, 0)),
                  pl.BlockSpec((k, tn), lambda j: (0, j)),
                  pl.BlockSpec((1, tn), lambda j: (0, j))],
        out_specs=pl.BlockSpec((m, tn), lambda j: (0, j)),
        out_shape=jax.ShapeDtypeStruct((m, n), F32),
        compiler_params=_cparams(("parallel",)),
        name="mm_small",
    )(a, w, b)


def _inproj_kernel(x_ref, g_ref, sh_ref, sc_ref, w_ref, o_ref, hx_ref):
    @pl.when(pl.program_id(1) == 0)
    def _():
        hx_ref[...] = _rms_mod(x_ref[...], g_ref[...], sh_ref[...], sc_ref[...]).astype(BF16)

    o_ref[...] = jnp.dot(hx_ref[...], w_ref[...], preferred_element_type=F32).astype(o_ref.dtype)


def _inproj(x, g, shift, scale, w, tm, tn):
    m, d = x.shape
    n = w.shape[1]
    row = lambda i, j: (0, 0)
    return pl.pallas_call(
        _inproj_kernel,
        grid=(m // tm, n // tn),
        in_specs=[pl.BlockSpec((tm, d), lambda i, j: (i, 0)),
                  pl.BlockSpec((1, d), row), pl.BlockSpec((1, d), row), pl.BlockSpec((1, d), row),
                  pl.BlockSpec((d, tn), lambda i, j: (0, j))],
        out_specs=pl.BlockSpec((tm, tn), lambda i, j: (i, j)),
        out_shape=jax.ShapeDtypeStruct((m, n), F32),
        scratch_shapes=[pltpu.VMEM((tm, d), BF16)],
        compiler_params=_cparams(("parallel", "arbitrary")),
        name="inproj",
    )(x, g, shift, scale, w)


def _ssd_kernel(*refs, reverse, final, direction):
    if final:
        (xs_ref, b_ref, c_ref, dt_ref, dtb_ref, a_ref, exp_ref, h0_ref,
         yp_ref, z_ref, dsk_ref, nrm_ref, y_ref, hfin_ref, st_ref) = refs
    else:
        (xs_ref, b_ref, c_ref, dt_ref, dtb_ref, a_ref, exp_ref, h0_ref, y_ref, hfin_ref, st_ref) = refs
    c = pl.program_id(0)
    nc = pl.num_programs(0)
    gw = BRANCH_W // SSM_GROUPS

    @pl.when(c == 0)
    def _():
        st_ref[...] = h0_ref[...]

    rows = lax.broadcasted_iota(jnp.int32, (CHUNK, CHUNK), 0)
    cols = lax.broadcasted_iota(jnp.int32, (CHUNK, CHUNK), 1)
    tri = (rows <= cols) if reverse else (rows >= cols)
    tri_bf = tri.astype(BF16)
    lane_lo = cols < SSM_HEAD_DIM
    end_row = 0 if reverse else CHUNK - 1

    dt_all = _softplus(dt_ref[...] + dtb_ref[...])
    dta = dt_all * a_ref[...]
    acum = _dot01_l(tri_bf, dta)
    acum_t = acum.T
    a_hi, a_lo = _split2(acum)
    dt_bf = dt_all.astype(BF16)

    for g in range(SSM_GROUPS):
        exp_m = exp_ref[g]
        acum_b = (jnp.dot(a_hi, exp_m, preferred_element_type=F32)
                  + jnp.dot(a_lo, exp_m, preferred_element_type=F32))
        dt_b = jnp.dot(dt_bf, exp_m, preferred_element_type=F32)
        xs = xs_ref[:, gw * g:gw * (g + 1)]
        bm = b_ref[:, SSM_STATE * g:SSM_STATE * (g + 1)]
        cm = c_ref[:, SSM_STATE * g:SSM_STATE * (g + 1)].astype(BF16)
        bm_t = bm.T.astype(BF16)
        scores = _dot_nt(cm, bm.astype(BF16))
        ys = []
        for j in range(SSM_HPG // 2):
            r0, r1 = 2 * j, 2 * j + 1
            col0 = direction * SSM_HEADS + g * SSM_HPG + r0
            ab0 = acum_b[:, LANES * r0:LANES * (r0 + 1)]
            ab1 = acum_b[:, LANES * r1:LANES * (r1 + 1)]
            acum_e = jnp.where(lane_lo, ab0, ab1)
            dt_e = jnp.where(lane_lo, dt_b[:, LANES * r0:LANES * (r0 + 1)], dt_b[:, LANES * r1:LANES * (r1 + 1)])
            xp = xs[:, LANES * j:LANES * (j + 1)] * dt_e
            end = acum_e[end_row:end_row + 1, :]
            xw = (xp * jnp.exp(end - acum_e)).astype(BF16)
            l0 = jnp.exp(jnp.where(tri, ab0 - acum_t[col0:col0 + 1, :], NEG_BIG))
            l1 = jnp.exp(jnp.where(tri, ab1 - acum_t[col0 + 1:col0 + 2, :], NEG_BIG))
            m0 = (scores * l0).astype(BF16)
            m1 = (scores * l1).astype(BF16)
            xp0 = jnp.where(lane_lo, xp, 0.0).astype(BF16)
            xp1 = jnp.where(lane_lo, 0.0, xp).astype(BF16)
            y_diag = (jnp.dot(m0, xp0, preferred_element_type=F32)
                      + jnp.dot(m1, xp1, preferred_element_type=F32))
            st = st_ref[g, j]
            y_off = jnp.dot(cm, st.astype(BF16), preferred_element_type=F32) * jnp.exp(acum_e)
            ys.append(y_diag + y_off)
            st_ref[g, j] = st * jnp.exp(end) + jnp.dot(bm_t, xw, preferred_element_type=F32)
        y = jnp.concatenate(ys, axis=1)
        if final:
            gs = slice(gw * g, gw * (g + 1))
            y = y + yp_ref[:, gs] + xs * dsk_ref[:, gs]
            y = y * _silu(z_ref[:, gs])
            y = y * lax.rsqrt(jnp.mean(y * y, axis=-1, keepdims=True) + EPS)
            y = y * nrm_ref[:, gs]
        y_ref[:, gw * g:gw * (g + 1)] = y.astype(y_ref.dtype)

    @pl.when(c == nc - 1)
    def _():
        hfin_ref[...] = st_ref[...]


def _ssd_tables(direction):
    exp_m = np.zeros((SSM_GROUPS, LANES, SSM_HPG * LANES), np.float32)
    for g in range(SSM_GROUPS):
        for r in range(SSM_HPG):
            col = direction * SSM_HEADS + g * SSM_HPG + r
            exp_m[g, col, LANES * r:LANES * (r + 1)] = 1.0
    return jnp.asarray(exp_m, BF16)


def _ssd_pass(xbc, u_all, dtb_row, a_row, h0, direction, final_args=None):
    L = xbc.shape[0]
    nc = L // CHUNK
    reverse = direction == 1
    final = final_args is not None
    exp_m = _ssd_tables(direction)
    bc_w = SSM_GROUPS * SSM_STATE
    cidx = (lambda c: nc - 1 - c) if reverse else (lambda c: c)
    st_shape = (SSM_GROUPS, SSM_HPG // 2, SSM_STATE, LANES)
    in_specs = [
        pl.BlockSpec((CHUNK, BRANCH_W), lambda c: (cidx(c), 0)),
        pl.BlockSpec((CHUNK, bc_w), lambda c: (cidx(c), BRANCH_W // bc_w)),
        pl.BlockSpec((CHUNK, bc_w), lambda c: (cidx(c), BRANCH_W // bc_w + 1)),
        pl.BlockSpec((CHUNK, LANES), lambda c: (cidx(c), OFF_DT // LANES)),
        pl.BlockSpec((1, LANES), lambda c: (0, 0)),
        pl.BlockSpec((1, LANES), lambda c: (0, 0)),
        pl.BlockSpec((SSM_GROUPS, LANES, SSM_HPG * LANES), lambda c: (0, 0, 0)),
        pl.BlockSpec(st_shape, lambda c: (0, 0, 0, 0)),
    ]
    args = [xbc, xbc, xbc, u_all, dtb_row, a_row, exp_m, h0]
    if final:
        y_prev, dsk_row, nrm_row = final_args
        in_specs += [
            pl.BlockSpec((CHUNK, BRANCH_W), lambda c: (cidx(c), 0)),
            pl.BlockSpec((CHUNK, BRANCH_W), lambda c: (cidx(c), OFF_Z // BRANCH_W)),
            pl.BlockSpec((1, BRANCH_W), lambda c: (0, 0)),
            pl.BlockSpec((1, BRANCH_W), lambda c: (0, 0)),
        ]
        args += [y_prev, u_all, dsk_row, nrm_row]
    out_dtype = BF16 if final else F32
    y, hfin = pl.pallas_call(
        functools.partial(_ssd_kernel, reverse=reverse, final=final, direction=direction),
        grid=(nc,),
        in_specs=in_specs,
        out_specs=[pl.BlockSpec((CHUNK, BRANCH_W), lambda c: (cidx(c), 0)),
                   pl.BlockSpec(st_shape, lambda c: (0, 0, 0, 0))],
        out_shape=[jax.ShapeDtypeStruct((L, BRANCH_W), out_dtype),
                   jax.ShapeDtypeStruct(st_shape, F32)],
        scratch_shapes=[pltpu.VMEM(st_shape, F32)],
        compiler_params=_cparams(("arbitrary",)),
        name="ssd_bwd" if reverse else "ssd_fwd",
    )(*args)
    return y, hfin


def _conv_silu(u_all, conv_w, conv_b):
    L = u_all.shape[0]
    u = u_all[:, OFF_XBC:OFF_XBC + XBC_W]
    up = jnp.pad(u, ((CONV_W // 2, CONV_W // 2), (0, 0)))
    acc = conv_b[None, :]
    for k in range(CONV_W):
        acc = acc + up[k:k + L] * conv_w[k][None, :]
    return acc * jax.nn.sigmoid(acc)


def _pad_row(v, width=LANES):
    return jnp.zeros((1, width), F32).at[0, :v.shape[0]].set(v.astype(F32))


def _mamba(u_all, conv_w, conv_b, dt_bias, a_log, d_skip, ssm_norm, h0_f, h0_b):
    xbc = _conv_silu(u_all, conv_w, conv_b)
    dtb_row = _pad_row(dt_bias.reshape(-1))
    a_row = _pad_row(-jnp.exp(a_log.astype(F32)).reshape(-1))
    dsk_row = jnp.repeat(d_skip.astype(F32), SSM_HEAD_DIM)[None, :]
    nrm_row = ssm_norm.astype(F32)[None, :]
    y_f, hf = _ssd_pass(xbc, u_all, dtb_row, a_row, h0_f, 0)
    y, hb = _ssd_pass(xbc, u_all, dtb_row, a_row, h0_b, 1, (y_f, dsk_row, nrm_row))
    return y, hf, hb


def _gla_chunk(uq, uf, v, lb, st_t, ebc, sel_small, bd_mask, reverse):
    Q = CHUNK
    rows = lax.broadcasted_iota(jnp.int32, (Q, LANES), 0)
    lanes = lax.broadcasted_iota(jnp.int32, (Q, LANES), 1)
    r2 = lax.broadcasted_iota(jnp.int32, (Q, Q), 0)
    c2 = lax.broadcasted_iota(jnp.int32, (Q, Q), 1)
    tri_bf = ((r2 <= c2) if reverse else (r2 >= c2)).astype(BF16)
    lane_lo = lanes < HGRN_KEY
    end_row = 0 if reverse else Q - 1

    q = _silu(uq)
    f = lb + (1.0 - lb) * _sigmoid(uf)
    logf = jnp.log(jnp.maximum(f, F_MIN))
    k = (1.0 - lb) * _sigmoid(-uf)
    gc = _dot01_l(tri_bf, logf)

    xr = r2 ^ c2
    att = jnp.dot((q * k).astype(BF16), ebc, preferred_element_type=F32)
    a0 = jnp.where(xr == 0, att[:, :HGRN_VAL], 0.0)
    a1 = jnp.where(xr == 0, att[:, HGRN_VAL:], 0.0)
    b = 1
    lvl = 0
    while b < Q:
        if 2 * b < SUBLANES:
            ref = _dot01_l(sel_small[lvl], gc)
        else:
            pieces = []
            for p0 in range(0, Q, 2 * b):
                row = p0 + b if reverse else p0 + b - 1
                pieces.append(jnp.broadcast_to(gc[row:row + 1, :], (2 * b, LANES)))
            ref = jnp.concatenate(pieces, axis=0) if len(pieces) > 1 else pieces[0]
        hi = (rows & b) != 0
        is_q = (~hi) if reverse else hi
        qt = (q * jnp.exp(jnp.where(is_q, gc - ref, NEG_BIG))).astype(BF16)
        ek = ref - gc
        kt0 = (k * jnp.exp(jnp.where((~is_q) & lane_lo, ek, NEG_BIG))).astype(BF16)
        kt1 = (k * jnp.exp(jnp.where((~is_q) & (~lane_lo), ek, NEG_BIG))).astype(BF16)
        att = _dot_nt(qt, jnp.concatenate([kt0, kt1], axis=0))
        if 2 * b < Q:
            same = xr < 2 * b
            a0 = jnp.where(same & (xr >= b), att[:, :Q], a0)
            a1 = jnp.where(same & (xr >= b), att[:, Q:], a1)
        else:
            a0 = jnp.where(xr >= b, att[:, :Q], a0)
            a1 = jnp.where(xr >= b, att[:, Q:], a1)
        b *= 2
        lvl += 1
    vb = v.astype(BF16)
    o = jnp.concatenate([jnp.dot(a0.astype(BF16), vb[:, :HGRN_VAL], preferred_element_type=F32),
                         jnp.dot(a1.astype(BF16), vb[:, HGRN_VAL:], preferred_element_type=F32)], axis=1)

    qg = (q * jnp.exp(gc)).astype(BF16)
    o = o + _dot_nt(qg, st_t.astype(BF16))
    end = gc[end_row:end_row + 1, :]
    kd = (k * jnp.exp(end - gc)).astype(BF16)
    st_new = st_t * jnp.exp(end) + jnp.dot(v.T.astype(BF16), kd, preferred_element_type=F32) * bd_mask
    return o, st_new


def _gla_kernel(u_hbm, lb_ref, ebc_ref, sel_ref, s0_ref, o_hbm, sfin_ref, st_ref, bq, bf, bv, bo, isem, osem,
                *, reverse, direction, ncol, nrow_chunks):
    p = pl.program_id(0)
    c = pl.program_id(1)
    col = (ncol - 1 - c) if reverse else c
    nxt = (col - 1) if reverse else (col + 1)
    slot = c % 2
    vw = 2 * HGRN_VAL
    q_off = pl.multiple_of(OFF_Q + p * LANES, LANES)
    f_off = pl.multiple_of(OFF_F + direction * HGRN_HK + p * LANES, LANES)
    v_off = pl.multiple_of(OFF_I + p * vw, LANES)
    o_off = pl.multiple_of(p * vw, LANES)

    def in_copies(col_, slot_):
        return (pltpu.make_async_copy(u_hbm.at[:, col_, pl.ds(q_off, LANES)], bq.at[slot_], isem.at[0, slot_]),
                pltpu.make_async_copy(u_hbm.at[:, col_, pl.ds(f_off, LANES)], bf.at[slot_], isem.at[1, slot_]),
                pltpu.make_async_copy(u_hbm.at[:, col_, pl.ds(v_off, vw)], bv.at[slot_], isem.at[2, slot_]))

    def out_copy(col_, slot_):
        return pltpu.make_async_copy(bo.at[slot_], o_hbm.at[:, col_, pl.ds(o_off, vw)], osem.at[slot_])

    @pl.when(c == 0)
    def _():
        st_ref[...] = s0_ref[...]
        for cp in in_copies(col, slot):
            cp.start()

    for cp in in_copies(col, slot):
        cp.wait()

    @pl.when(c + 1 < ncol)
    def _():
        for cp in in_copies(nxt, 1 - slot):
            cp.start()

    @pl.when(c >= 2)
    def _():
        out_copy(col, slot).wait()

    r2 = lax.broadcasted_iota(jnp.int32, (vw, LANES), 0)
    c2 = lax.broadcasted_iota(jnp.int32, (vw, LANES), 1)
    bd_mask = ((r2 < HGRN_VAL) == (c2 < HGRN_KEY)).astype(F32)
    lb = lb_ref[...]
    ebc = ebc_ref[...]
    sel_small = [sel_ref[i] for i in range(sel_ref.shape[0])]
    order = range(nrow_chunks - 1, -1, -1) if reverse else range(nrow_chunks)
    for rc in order:
        sl = pl.ds(rc * CHUNK, CHUNK)
        o, st_new = _gla_chunk(bq[slot, sl, :], bf[slot, sl, :], bv[slot, sl, :], lb, st_ref[...], ebc, sel_small,
                               bd_mask, reverse)
        st_ref[...] = st_new
        bo[slot, sl, :] = o

    out_copy(col, slot).start()

    @pl.when(c == ncol - 1)
    def _():
        out_copy(col, slot).wait()
        if ncol >= 2:
            out_copy(col, 1 - slot).wait()
        sfin_ref[...] = st_ref[...]


def _gla_pass(u3, lb_rows, s0, direction):
    R, Wc, _ = u3.shape
    reverse = direction == 1
    ebc = np.zeros((LANES, 2 * HGRN_VAL), np.float32)
    ebc[:HGRN_KEY, :HGRN_VAL] = 1.0
    ebc[HGRN_KEY:, HGRN_VAL:] = 1.0
    ebc = jnp.asarray(ebc, BF16)
    small = [b for b in (1, 2, 4) if 2 * b < SUBLANES]
    sel = np.zeros((len(small), CHUNK, CHUNK), np.float32)
    for i, b in enumerate(small):
        for r in range(CHUNK):
            p0 = (r // (2 * b)) * 2 * b
            sel[i, r, p0 + b if reverse else p0 + b - 1] = 1.0
    sel = jnp.asarray(sel, BF16)
    vw = 2 * HGRN_VAL
    o, sfin = pl.pallas_call(
        functools.partial(_gla_kernel, reverse=reverse, direction=direction, ncol=Wc, nrow_chunks=R // CHUNK),
        grid=(HGRN_PAIRS, Wc),
        in_specs=[pl.BlockSpec(memory_space=pl.ANY),
                  pl.BlockSpec((None, 1, LANES), lambda p, c: (p, 0, 0)),
                  pl.BlockSpec((LANES, vw), lambda p, c: (0, 0)),
                  pl.BlockSpec((len(small), CHUNK, CHUNK), lambda p, c: (0, 0, 0)),
                  pl.BlockSpec((None, vw, LANES), lambda p, c: (p, 0, 0))],
        out_specs=[pl.BlockSpec(memory_space=pl.ANY),
                   pl.BlockSpec((None, vw, LANES), lambda p, c: (p, 0, 0))],
        out_shape=[jax.ShapeDtypeStruct((R, Wc, BRANCH_W), F32),
                   jax.ShapeDtypeStruct((HGRN_PAIRS, vw, LANES), F32)],
        scratch_shapes=[pltpu.VMEM((vw, LANES), F32),
                        pltpu.VMEM((2, R, LANES), F32), pltpu.VMEM((2, R, LANES), F32),
                        pltpu.VMEM((2, R, vw), F32), pltpu.VMEM((2, R, vw), F32),
                        pltpu.SemaphoreType.DMA((3, 2)), pltpu.SemaphoreType.DMA((2,))],
        compiler_params=_cparams(("arbitrary", "arbitrary")),
        name="gla_bwd" if reverse else "gla_fwd",
    )(u3, lb_rows, ebc, sel, s0)
    return o, sfin


def _hgrn_post_kernel(of_ref, ob_ref, g_ref, nrm_ref, y_ref):
    o = of_ref[...] + ob_ref[...]
    gate = _silu(g_ref[...])
    nrm = nrm_ref[...]
    for h in range(2):
        sl = slice(HGRN_VAL * h, HGRN_VAL * (h + 1))
        oh = o[:, sl]
        oh = oh * lax.rsqrt(jnp.mean(oh * oh, axis=-1, keepdims=True) + EPS)
        y_ref[:, sl] = (oh * nrm * gate[:, sl]).astype(y_ref.dtype)


def _hgrn_post(o_f, o_b, u_all, nrm_row, tm):
    L = o_f.shape[0]
    vw = 2 * HGRN_VAL
    blk = pl.BlockSpec((tm, vw), lambda i, p: (i, p))
    return pl.pallas_call(
        _hgrn_post_kernel,
        grid=(L // tm, HGRN_PAIRS),
        in_specs=[blk, blk, pl.BlockSpec((tm, vw), lambda i, p: (i, OFF_G // vw + p)),
                  pl.BlockSpec((1, HGRN_VAL), lambda i, p: (0, 0))],
        out_specs=blk,
        out_shape=jax.ShapeDtypeStruct((L, BRANCH_W), BF16),
        compiler_params=_cparams(("parallel", "parallel")),
        name="hgrn_post",
    )(o_f, o_b, u_all, nrm_row)


def _hgrn(u_all, lower_bound, hgrn_norm, s0_f, s0_b, grid_w):
    L = u_all.shape[0]
    R = L // grid_w
    u3 = u_all.reshape(R, grid_w, N_PAD)
    lb = lower_bound.astype(F32).reshape(2, HGRN_PAIRS, 1, LANES)
    o_f, sf = _gla_pass(u3, lb[0], s0_f, 0)
    o_b, sb = _gla_pass(u3, lb[1], s0_b, 1)
    y = _hgrn_post(o_f.reshape(L, BRANCH_W), o_b.reshape(L, BRANCH_W), u_all, hgrn_norm.astype(F32)[None, :],
                   min(L, 1024))
    return y, sf, sb


def _dft_tables(R, Wc):
    L = R * Wc
    q = np.arange(R)[None, :, None]
    r = np.arange(R)[None, None, :]
    c = np.arange(Wc)[:, None, None]
    ang = 2.0 * np.pi * ((q * (Wc * r + c)) % L) / L
    m = np.concatenate([np.cos(ang), -np.sin(ang)], axis=1)
    qc = np.arange(Wc)[:, None]
    cc = np.arange(Wc)[None, :]
    a2 = 2.0 * np.pi * ((qc * cc) % Wc) / Wc
    t = np.block([[np.cos(a2), np.sin(a2)], [np.sin(a2), -np.cos(a2)]])
    ch = np.arange(FOURIER_GW)
    a3 = 2.0 * np.pi * ((ch[:, None] * ch[None, :]) % FOURIER_GW) / FOURIER_GW
    cs = np.concatenate([np.cos(a3), -np.sin(a3)], axis=0) / math.sqrt(L * FOURIER_GW)
    return (jnp.asarray(m, F32).astype(BF16), jnp.asarray(t, F32).astype(BF16), jnp.asarray(cs, F32).astype(BF16))


def _fnet_a_kernel(x_ref, m_ref, y_ref, *, wb, R):
    for jj in range(wb):
        res = jnp.dot(m_ref[jj], x_ref[:, jj, :].astype(BF16), preferred_element_type=F32)
        y_ref[:, 0, jj, :] = res[:R]
        y_ref[:, 1, jj, :] = res[R:]


def _fnet_b_kernel(y_ref, t_ref, cs_ref, o_ref, *, Wc, rb):
    t = t_ref[...]
    cs = cs_ref[...]
    for i in range(rb):
        yy = y_ref[i].reshape(2 * Wc, FOURIER_GW).astype(BF16)
        pq = jnp.dot(t, yy, preferred_element_type=F32)
        lhs = jnp.concatenate([pq[:Wc], pq[Wc:]], axis=1).astype(BF16)
        o_ref[:, i, :] = jnp.dot(lhs, cs, preferred_element_type=F32).astype(o_ref.dtype)


def _fnet(u_all, grid_w):
    L = u_all.shape[0]
    R = L // grid_w
    Wc = grid_w
    wb = SUBLANES
    rb = SUBLANES
    m, t, cs = _dft_tables(R, Wc)
    u3 = u_all.reshape(R, Wc, N_PAD)
    y = pl.pallas_call(
        functools.partial(_fnet_a_kernel, wb=wb, R=R),
        grid=(Wc // wb, FOURIER_GROUPS),
        in_specs=[pl.BlockSpec((R, wb, FOURIER_GW), lambda cb, g: (0, cb, OFF_FOUR // FOURIER_GW + g)),
                  pl.BlockSpec((wb, 2 * R, R), lambda cb, g: (cb, 0, 0))],
        out_specs=pl.BlockSpec((R, 2, wb, FOURIER_GW), lambda cb, g: (0, 0, cb, g)),
        out_shape=jax.ShapeDtypeStruct((R, 2, Wc, BRANCH_W), F32),
        compiler_params=_cparams(("parallel", "arbitrary")),
        name="fnet_a",
    )(u3, m)
    out = pl.pallas_call(
        functools.partial(_fnet_b_kernel, Wc=Wc, rb=rb),
        grid=(R // rb, FOURIER_GROUPS),
        in_specs=[pl.BlockSpec((rb, 2, Wc, FOURIER_GW), lambda qb, g: (qb, 0, 0, g)),
                  pl.BlockSpec((2 * Wc, 2 * Wc), lambda qb, g: (0, 0)),
                  pl.BlockSpec((2 * FOURIER_GW, FOURIER_GW), lambda qb, g: (0, 0))],
        out_specs=pl.BlockSpec((Wc, rb, FOURIER_GW), lambda qb, g: (0, qb, g)),
        out_shape=jax.ShapeDtypeStruct((Wc, R, BRANCH_W), BF16),
        compiler_params=_cparams(("parallel", "arbitrary")),
        name="fnet_b",
    )(y, t, cs)
    return out.reshape(L, BRANCH_W)


def _fnet_small_kernel(x_ref, m_ref, cs_ref, o_ref, *, L):
    pq = jnp.dot(m_ref[...], x_ref[...].astype(BF16), preferred_element_type=F32)
    lhs = jnp.concatenate([pq[:L], -pq[L:]], axis=1).astype(BF16)
    o_ref[...] = jnp.dot(lhs, cs_ref[...], preferred_element_type=F32).astype(o_ref.dtype)


def _fnet_small(u_all):
    L = u_all.shape[0]
    m, _, cs = _dft_tables(L, 1)
    return pl.pallas_call(
        functools.partial(_fnet_small_kernel, L=L),
        grid=(FOURIER_GROUPS,),
        in_specs=[pl.BlockSpec((L, FOURIER_GW), lambda g: (0, OFF_FOUR // FOURIER_GW + g)),
                  pl.BlockSpec((2 * L, L), lambda g: (0, 0)),
                  pl.BlockSpec((2 * FOURIER_GW, FOURIER_GW), lambda g: (0, 0))],
        out_specs=pl.BlockSpec((L, FOURIER_GW), lambda g: (0, g)),
        out_shape=jax.ShapeDtypeStruct((L, BRANCH_W), BF16),
        compiler_params=_cparams(("parallel",)),
        name="fnet_small",
    )(u_all, m[0], cs)


def _merge_kernel(ym_ref, yh_ref, yf_ref, ug_ref, gu_ref, gb_ref, wbr_ref, wo_ref, x_ref, g1_ref, gt_ref,
                  o_ref, mg_ref, acc_ref, ss_ref, *, nb, tn):
    j = pl.program_id(1)

    @pl.when(j < nb)
    def _():
        ug = ug_ref[...].astype(BF16)
        tot = None
        for i, y_ref in enumerate((ym_ref, yh_ref, yf_ref)):
            gate = _sigmoid(jnp.dot(ug, gu_ref[i], preferred_element_type=F32) + gb_ref[i])
            term = gate * jnp.dot(y_ref[...], wbr_ref[i], preferred_element_type=F32)
            tot = term if tot is None else tot + term
        mg_ref[j] = tot.astype(BF16)

    @pl.when((j >= nb) & (j < 2 * nb))
    def _():
        acc = jnp.dot(mg_ref[0], wo_ref[0:tn, :], preferred_element_type=F32)
        for kk in range(1, nb):
            acc = acc + jnp.dot(mg_ref[kk], wo_ref[kk * tn:(kk + 1) * tn, :], preferred_element_type=F32)
        acc_ref[j - nb] = acc
        part = jnp.sum(acc * acc, axis=-1, keepdims=True)

        @pl.when(j == nb)
        def _():
            ss_ref[...] = part

        @pl.when(j > nb)
        def _():
            ss_ref[...] = ss_ref[...] + part

    @pl.when(j >= 2 * nb)
    def _():
        inv = lax.rsqrt(ss_ref[...] * (1.0 / (nb * tn)) + EPS)
        o_ref[...] = x_ref[...] + gt_ref[...] * ((acc_ref[j - 2 * nb] * inv) * g1_ref[...])


def _merge(ym, yh, yf, u_all, gate_up, gate_b, w_branch, w_out, x, g1, gate, tm, tn):
    m, d = x.shape
    nb = d // tn
    p1 = lambda i, j: jnp.minimum(j, nb - 1)
    p2 = lambda i, j: jnp.clip(j - nb, 0, nb - 1)
    p3 = lambda i, j: jnp.clip(j - 2 * nb, 0, nb - 1)
    ybs = pl.BlockSpec((tm, BRANCH_W), lambda i, j: (i, 0))
    return pl.pallas_call(
        functools.partial(_merge_kernel, nb=nb, tn=tn),
        grid=(m // tm, 3 * nb),
        in_specs=[ybs, ybs, ybs,
                  pl.BlockSpec((tm, GATE_RANK), lambda i, j: (i, OFF_GATE // GATE_RANK)),
                  pl.BlockSpec((3, GATE_RANK, tn), lambda i, j: (0, 0, p1(i, j))),
                  pl.BlockSpec((3, 1, tn), lambda i, j: (0, 0, p1(i, j))),
                  pl.BlockSpec((3, BRANCH_W, tn), lambda i, j: (0, 0, p1(i, j))),
                  pl.BlockSpec((d, tn), lambda i, j: (0, p2(i, j))),
                  pl.BlockSpec((tm, tn), lambda i, j: (i, p3(i, j))),
                  pl.BlockSpec((1, tn), lambda i, j: (0, p3(i, j))),
                  pl.BlockSpec((1, tn), lambda i, j: (0, p3(i, j)))],
        out_specs=pl.BlockSpec((tm, tn), lambda i, j: (i, p3(i, j))),
        out_shape=jax.ShapeDtypeStruct((m, d), F32),
        scratch_shapes=[pltpu.VMEM((nb, tm, tn), BF16), pltpu.VMEM((nb, tm, tn), F32), pltpu.VMEM((tm, 1), F32)],
        compiler_params=_cparams(("parallel", "arbitrary")),
        name="merge",
    )(ym, yh, yf, u_all, gate_up, gate_b, w_branch, w_out, x, g1, gate)


def _mlp_kernel(x_ref, g_ref, sh_ref, sc_ref, wu_ref, wd_ref, xb_ref, g3_ref, gt_ref, o_ref,
                hx_ref, h0_ref, h1_ref, acc_ref, inv_ref, *, nh, nb, tn):
    j = pl.program_id(1)

    def up(h_ref):
        h = jnp.dot(hx_ref[...], wu_ref[...], preferred_element_type=F32)
        h_ref[...] = jnp.square(jnp.maximum(h, 0.0)).astype(BF16)

    def down(h_ref):
        h = h_ref[...]
        for kk in range(nb):
            acc_ref[kk] = acc_ref[kk] + jnp.dot(h, wd_ref[:, kk * tn:(kk + 1) * tn], preferred_element_type=F32)

    @pl.when(j == 0)
    def _():
        hx_ref[...] = _rms_mod(x_ref[...], g_ref[...], sh_ref[...], sc_ref[...]).astype(BF16)
        acc_ref[...] = jnp.zeros(acc_ref.shape, F32)
        up(h0_ref)

    @pl.when((j > 0) & (j < nh) & (j % 2 == 1))
    def _():
        down(h0_ref)
        up(h1_ref)

    @pl.when((j > 0) & (j < nh) & (j % 2 == 0))
    def _():
        down(h1_ref)
        up(h0_ref)

    @pl.when(j == nh)
    def _():
        down(h1_ref if (nh - 1) % 2 == 1 else h0_ref)
        ss = jnp.sum(jnp.square(acc_ref[0]), axis=-1, keepdims=True)
        for kk in range(1, nb):
            ss = ss + jnp.sum(jnp.square(acc_ref[kk]), axis=-1, keepdims=True)
        inv_ref[...] = lax.rsqrt(ss * (1.0 / (nb * tn)) + EPS)

    @pl.when(j > nh)
    def _():
        o_ref[...] = xb_ref[...] + gt_ref[...] * ((acc_ref[j - nh - 1] * inv_ref[...]) * g3_ref[...])


def _mlp(x, g2, shift, scale, w_up, w_down, g3, gate, tm, th, tn):
    m, d = x.shape
    dff = w_up.shape[1]
    nh = dff // th
    nb = d // tn
    row = lambda i, j: (0, 0)
    ph = lambda i, j: jnp.minimum(j, nh - 1)
    pd = lambda i, j: jnp.clip(j - 1, 0, nh - 1)
    pe = lambda i, j: jnp.clip(j - nh - 1, 0, nb - 1)
    return pl.pallas_call(
        functools.partial(_mlp_kernel, nh=nh, nb=nb, tn=tn),
        grid=(m // tm, nh + 1 + nb),
        in_specs=[pl.BlockSpec((tm, d), lambda i, j: (i, 0), pipeline_mode=pl.Buffered(1)),
                  pl.BlockSpec((1, d), row), pl.BlockSpec((1, d), row), pl.BlockSpec((1, d), row),
                  pl.BlockSpec((d, th), lambda i, j: (0, ph(i, j))),
                  pl.BlockSpec((th, d), lambda i, j: (pd(i, j), 0)),
                  pl.BlockSpec((tm, tn), lambda i, j: (i, pe(i, j))),
                  pl.BlockSpec((1, tn), lambda i, j: (0, pe(i, j))),
                  pl.BlockSpec((1, tn), lambda i, j: (0, pe(i, j)))],
        out_specs=pl.BlockSpec((tm, tn), lambda i, j: (i, pe(i, j))),
        out_shape=jax.ShapeDtypeStruct((m, d), F32),
        scratch_shapes=[pltpu.VMEM((tm, d), BF16), pltpu.VMEM((tm, th), BF16), pltpu.VMEM((tm, th), BF16),
                        pltpu.VMEM((nb, tm, tn), F32), pltpu.VMEM((tm, 1), F32)],
        compiler_params=_cparams(("parallel", "arbitrary")),
        name="mlp",
    )(x, g2, shift, scale, w_up, w_down, x, g3, gate)


_DT_SHIFT = IN_SPLITS[2]


def _wprep_table():
    src, mode = [], []
    nblk = lambda w: w // LANES
    add = lambda a0, n, m: (src.extend(range(a0, a0 + n)), mode.extend([m] * n))
    add(0, nblk(BRANCH_W), 1)
    add((sum(IN_SPLITS[:7]) - _DT_SHIFT) // LANES, nblk(BRANCH_W), 0)
    add(BRANCH_W // LANES, nblk(XBC_W), 1)
    add((BRANCH_W + XBC_W) // LANES, nblk(HGRN_HK + 2 * HGRN_HK + 2 * BRANCH_W), 0)
    add((sum(IN_SPLITS[:8]) - _DT_SHIFT) // LANES, nblk(GATE_RANK), 0)
    add((BRANCH_W + XBC_W) // LANES, 1, 2)
    while len(src) < N_PAD // LANES:
        add(0, 1, 3)
    return np.asarray(src, np.int32), np.asarray(mode, np.int32)


def _wprep_kernel(src_ref, mode_ref, a_ref, b_ref, o_ref):
    mode = mode_ref[pl.program_id(0)]
    lanes = lax.broadcasted_iota(jnp.int32, a_ref.shape, 1)

    @pl.when(mode == 0)
    def _():
        ra = pltpu.roll(a_ref[...], LANES - _DT_SHIFT, 1)
        rb = pltpu.roll(b_ref[...], LANES - _DT_SHIFT, 1)
        o_ref[...] = jnp.where(lanes < LANES - _DT_SHIFT, ra, rb).astype(o_ref.dtype)

    @pl.when(mode == 1)
    def _():
        o_ref[...] = a_ref[...].astype(o_ref.dtype)

    @pl.when(mode == 2)
    def _():
        o_ref[...] = jnp.where(lanes < _DT_SHIFT, a_ref[...], 0.0).astype(o_ref.dtype)

    @pl.when(mode == 3)
    def _():
        o_ref[...] = jnp.zeros(o_ref.shape, o_ref.dtype)


def _permute_w_in(w):
    d, n = w.shape
    src, mode = _wprep_table()
    last = (n - 1) // LANES
    return pl.pallas_call(
        _wprep_kernel,
        grid_spec=pltpu.PrefetchScalarGridSpec(
            num_scalar_prefetch=2,
            grid=(N_PAD // LANES,),
            in_specs=[pl.BlockSpec((d, LANES), lambda j, src, mode: (0, src[j])),
                      pl.BlockSpec((d, LANES), lambda j, src, mode: (0, jnp.minimum(src[j] + 1, last)))],
            out_specs=pl.BlockSpec((d, LANES), lambda j, src, mode: (0, j))),
        out_shape=jax.ShapeDtypeStruct((d, N_PAD), BF16),
        compiler_params=_cparams(("arbitrary",)),
        name="wprep",
    )(jnp.asarray(src), jnp.asarray(mode), w, w)


def _state_zeros():
    hm0 = jnp.zeros((SSM_GROUPS, SSM_HPG // 2, SSM_STATE, LANES), F32)
    sh0 = jnp.zeros((HGRN_PAIRS, 2 * HGRN_VAL, LANES), F32)
    return hm0, sh0


def kernel(x, c, ctx, c_ctx, ada_down, ada_up, ada_b, norm_g, w_in, conv_w, conv_b, ssm_dt_bias, ssm_a_log,
           ssm_d, ssm_norm, hgrn_lb_logits, hgrn_norm, gate_up, gate_b, w_branch, w_out, mlp_up, mlp_down):
    assert x.shape[0] == 1 and ctx.shape[0] == 1
    xs = x[0]
    cs = ctx[0]
    L, d = xs.shape
    lc = cs.shape[0]
    tm_x, tm_c = 512, lc
    p_lb = jax.nn.softmax(hgrn_lb_logits.astype(F32), axis=0)
    lower_bounds = jnp.clip(jnp.cumsum(p_lb, axis=0) - p_lb[0], 0.0, 1.0)
    cond = jnp.zeros((SUBLANES, d), F32).at[0].set(c[0]).at[1].set(c_ctx)
    cond = cond * jax.nn.sigmoid(cond)
    hm0, sh0 = _state_zeros()
    zero_b = jnp.zeros((1, ada_down.shape[-1]), F32)
    for l in range(DEPTH):
        last = l == DEPTH - 1
        g = norm_g[l].astype(F32)
        mods = _mm_small(_mm_small(cond, ada_down[l], zero_b, ada_down.shape[-1]), ada_up[l], ada_b[l][None, :], 2048)
        sx1, cx1, gx1, sx2, cx2, gx2 = [mods[0:1, i * d:(i + 1) * d] for i in range(6)]
        sc1, cc1, gc1, sc2, cc2, gc2 = [mods[1:2, i * d:(i + 1) * d] for i in range(6)]
        w_in_l = _permute_w_in(w_in[l])
        gu_l, gb_l = gate_up[l].astype(BF16), gate_b[l].astype(F32)[:, None, :]
        wbr_l, wo_l = w_branch[l].astype(BF16), w_out[l].astype(BF16)
        wu_l, wd_l = mlp_up[l].astype(BF16), mlp_down[l].astype(BF16)
        mamba_p = (conv_w[l], conv_b[l], ssm_dt_bias[l], ssm_a_log[l], ssm_d[l], ssm_norm[l])

        uc = _inproj(cs, g[0:1], sc1, cc1, w_in_l, tm_c, 768)
        ym_c, hm_f, hm_b = _mamba(uc, *mamba_p, hm0, hm0)
        yh_c, sh_f, sh_b = _hgrn(uc, lower_bounds[l], hgrn_norm[l], sh0, sh0, 1)
        ux = _inproj(xs, g[0:1], sx1, cx1, w_in_l, tm_x, 768)
        ym, _, _ = _mamba(ux, *mamba_p, hm_f, hm_b)
        yh, _, _ = _hgrn(ux, lower_bounds[l], hgrn_norm[l], sh_f, sh_b, GRID_W)
        yf = _fnet(ux, GRID_W)
        xs = _merge(ym, yh, yf, ux, gu_l, gb_l, wbr_l, wo_l, xs, g[1:2], gx1, tm_x, 512)
        xs = _mlp(xs, g[2:3], sx2, cx2, wu_l, wd_l, g[3:4], gx2, tm_x, 512, 512)
        if not last:
            yf_c = _fnet_small(uc)
            cs = _merge(ym_c, yh_c, yf_c, uc, gu_l, gb_l, wbr_l, wo_l, cs, g[1:2], gc1, tm_c, 512)
            cs = _mlp(cs, g[2:3], sc2, cc2, wu_l, wd_l, g[3:4], gc2, tm_c, 512, 512)
    return xs[None]
```

```python
import functools
import math

import numpy as np
import jax
import jax.numpy as jnp
from jax import lax
from jax.experimental import pallas as pl
from jax.experimental.pallas import tpu as pltpu

F32 = jnp.float32
BF16 = jnp.bfloat16

D_MODEL = 4096
DEPTH = 4
GRID_W = 64
BRANCH_W = 1536
SSM_HEAD_DIM = 64
SSM_HEADS = 24
SSM_GROUPS = 4
SSM_HPG = 6
SSM_STATE = 128
CONV_W = 5
XBC_W = BRANCH_W + 2 * SSM_GROUPS * SSM_STATE
HGRN_VAL = 128
HGRN_HEADS = 12
HGRN_KEY = 64
HGRN_HK = HGRN_HEADS * HGRN_KEY
HGRN_PAIRS = HGRN_HEADS // 2
FOURIER_GROUPS = 4
FOURIER_GW = BRANCH_W // FOURIER_GROUPS
GATE_RANK = 256
D_FF = 4 * D_MODEL
CHUNK = 128
EPS = 1e-6
F_MIN = 1e-30
IN_SPLITS = (BRANCH_W, XBC_W, 2 * SSM_HEADS, HGRN_HK, 2 * HGRN_HK, BRANCH_W, BRANCH_W, BRANCH_W, GATE_RANK)

LANES = 128
SUBLANES = 8
VMEM_LIMIT = 56 * 1024 * 1024

OFF_Z = 0
OFF_FOUR = OFF_Z + BRANCH_W
OFF_XBC = OFF_FOUR + BRANCH_W
OFF_Q = OFF_XBC + XBC_W
OFF_F = OFF_Q + HGRN_HK
OFF_I = OFF_F + 2 * HGRN_HK
OFF_G = OFF_I + BRANCH_W
OFF_GATE = OFF_G + BRANCH_W
OFF_DT = OFF_GATE + GATE_RANK
N_PAD = 11520
NEG_BIG = -1e30


def _cparams(sem):
    return pltpu.CompilerParams(dimension_semantics=sem, vmem_limit_bytes=VMEM_LIMIT)


def _sigmoid(x):
    return 1.0 / (1.0 + jnp.exp(-x))


def _silu(x):
    return x * _sigmoid(x)


def _softplus(x):
    return jnp.maximum(x, 0.0) + jnp.log1p(jnp.exp(-jnp.abs(x)))


def _split3(x):
    hi = x.astype(BF16)
    r1 = x - hi.astype(F32)
    mid = r1.astype(BF16)
    lo = (r1 - mid.astype(F32)).astype(BF16)
    return hi, mid, lo


def _split2(x):
    hi = x.astype(BF16)
    return hi, (x - hi.astype(F32)).astype(BF16)


def _dot01_l(m01, x):
    hi, mid, lo = _split3(x)
    d = lambda a: jnp.dot(m01, a, preferred_element_type=F32)
    return d(hi) + d(mid) + d(lo)


def _dot_nt(a, b):
    return lax.dot_general(a, b, (((1,), (1,)), ((), ())), preferred_element_type=F32)


def _rms_mod(x, g, shift, scale):
    ms = jnp.mean(x * x, axis=-1, keepdims=True)
    y = (x * lax.rsqrt(ms + EPS)) * g
    return y * (1.0 + scale) + shift


def _mm_small_kernel(a_ref, w_ref, b_ref, o_ref):
    o_ref[...] = jnp.dot(a_ref[...].astype(BF16), w_ref[...].astype(BF16),
                         preferred_element_type=F32) + b_ref[...]


def _mm_small(a, w, b, tn, layer):
    m, k = a.shape
    n = w.shape[2]
    return pl.pallas_call(
        _mm_small_kernel,
        grid=(n // tn,),
        in_specs=[pl.BlockSpec((m, k), lambda j: (0, 0)),
                  pl.BlockSpec((None, k, tn), lambda j: (layer, 0, j)),
                  pl.BlockSpec((1, tn), lambda j: (0, j))],
        out_specs=pl.BlockSpec((m, tn), lambda j: (0, j)),
        out_shape=jax.ShapeDtypeStruct((m, n), F32),
        compiler_params=_cparams(("parallel",)),
        name="mm_small",
    )(a, w, b)


def _inproj_kernel(x_ref, g_ref, sh_ref, sc_ref, w_ref, o_ref, hx_ref):
    @pl.when(pl.program_id(1) == 0)
    def _():
        hx_ref[...] = _rms_mod(x_ref[...], g_ref[...], sh_ref[...], sc_ref[...]).astype(BF16)

    o_ref[...] = jnp.dot(hx_ref[...], w_ref[...], preferred_element_type=F32).astype(o_ref.dtype)


def _inproj(x, g, shift, scale, w, tm, tn):
    m, d = x.shape
    n = w.shape[1]
    row = lambda i, j: (0, 0)
    return pl.pallas_call(
        _inproj_kernel,
        grid=(m // tm, n // tn),
        in_specs=[pl.BlockSpec((tm, d), lambda i, j: (i, 0)),
                  pl.BlockSpec((1, d), row), pl.BlockSpec((1, d), row), pl.BlockSpec((1, d), row),
                  pl.BlockSpec((d, tn), lambda i, j: (0, j))],
        out_specs=pl.BlockSpec((tm, tn), lambda i, j: (i, j)),
        out_shape=jax.ShapeDtypeStruct((m, n), F32),
        scratch_shapes=[pltpu.VMEM((tm, d), BF16)],
        compiler_params=_cparams(("parallel", "arbitrary")),
        name="inproj",
    )(x, g, shift, scale, w)


def _ssd_kernel(*refs, reverse, final, direction):
    if final:
        (xs_ref, b_ref, c_ref, dt_ref, dtb_ref, a_ref, exp_ref, h0_ref,
         yp_ref, z_ref, dsk_ref, nrm_ref, y_ref, hfin_ref, st_ref) = refs
    else:
        (xs_ref, b_ref, c_ref, dt_ref, dtb_ref, a_ref, exp_ref, h0_ref, y_ref, hfin_ref, st_ref) = refs
    c = pl.program_id(0)
    nc = pl.num_programs(0)
    gw = BRANCH_W // SSM_GROUPS

    @pl.when(c == 0)
    def _():
        st_ref[...] = h0_ref[...]

    rows = lax.broadcasted_iota(jnp.int32, (CHUNK, CHUNK), 0)
    cols = lax.broadcasted_iota(jnp.int32, (CHUNK, CHUNK), 1)
    tri = (rows <= cols) if reverse else (rows >= cols)
    tri_bf = tri.astype(BF16)
    lane_lo = cols < SSM_HEAD_DIM
    end_row = 0 if reverse else CHUNK - 1

    dt_all = _softplus(dt_ref[...] + dtb_ref[...])
    dta = dt_all * a_ref[...]
    acum = _dot01_l(tri_bf, dta)
    acum_t = acum.T
    a_hi, a_lo = _split2(acum)
    dt_bf = dt_all.astype(BF16)

    for g in range(SSM_GROUPS):
        exp_m = exp_ref[g]
        acum_b = (jnp.dot(a_hi, exp_m, preferred_element_type=F32)
                  + jnp.dot(a_lo, exp_m, preferred_element_type=F32))
        dt_b = jnp.dot(dt_bf, exp_m, preferred_element_type=F32)
        xs = xs_ref[:, gw * g:gw * (g + 1)].astype(F32)
        bm = b_ref[:, SSM_STATE * g:SSM_STATE * (g + 1)]
        cm = c_ref[:, SSM_STATE * g:SSM_STATE * (g + 1)]
        bm_t = bm.astype(F32).T.astype(BF16)
        scores = _dot_nt(cm, bm)
        ys = []
        for j in range(SSM_HPG // 2):
            r0, r1 = 2 * j, 2 * j + 1
            col0 = direction * SSM_HEADS + g * SSM_HPG + r0
            ab0 = acum_b[:, LANES * r0:LANES * (r0 + 1)]
            ab1 = acum_b[:, LANES * r1:LANES * (r1 + 1)]
            acum_e = jnp.where(lane_lo, ab0, ab1)
            dt_e = jnp.where(lane_lo, dt_b[:, LANES * r0:LANES * (r0 + 1)], dt_b[:, LANES * r1:LANES * (r1 + 1)])
            xp = xs[:, LANES * j:LANES * (j + 1)] * dt_e
            end = acum_e[end_row:end_row + 1, :]
            xw = (xp * jnp.exp(end - acum_e)).astype(BF16)
            l0 = jnp.exp(jnp.where(tri, ab0 - acum_t[col0:col0 + 1, :], NEG_BIG))
            l1 = jnp.exp(jnp.where(tri, ab1 - acum_t[col0 + 1:col0 + 2, :], NEG_BIG))
            m0 = (scores * l0).astype(BF16)
            m1 = (scores * l1).astype(BF16)
            xp0 = jnp.where(lane_lo, xp, 0.0).astype(BF16)
            xp1 = jnp.where(lane_lo, 0.0, xp).astype(BF16)
            y_diag = (jnp.dot(m0, xp0, preferred_element_type=F32)
                      + jnp.dot(m1, xp1, preferred_element_type=F32))
            st = st_ref[g, j]
            y_off = jnp.dot(cm, st.astype(BF16), preferred_element_type=F32) * jnp.exp(acum_e)
            ys.append(y_diag + y_off)
            st_ref[g, j] = st * jnp.exp(end) + jnp.dot(bm_t, xw, preferred_element_type=F32)
        y = jnp.concatenate(ys, axis=1)
        if final:
            gs = slice(gw * g, gw * (g + 1))
            y = y + yp_ref[:, gs] + xs * dsk_ref[:, gs]
            y = y * _silu(z_ref[:, gs])
            y = y * lax.rsqrt(jnp.mean(y * y, axis=-1, keepdims=True) + EPS)
            y = y * nrm_ref[:, gs]
        y_ref[:, gw * g:gw * (g + 1)] = y.astype(y_ref.dtype)

    @pl.when(c == nc - 1)
    def _():
        hfin_ref[...] = st_ref[...]


def _ssd_tables(direction):
    exp_m = np.zeros((SSM_GROUPS, LANES, SSM_HPG * LANES), np.float32)
    for g in range(SSM_GROUPS):
        for r in range(SSM_HPG):
            col = direction * SSM_HEADS + g * SSM_HPG + r
            exp_m[g, col, LANES * r:LANES * (r + 1)] = 1.0
    return jnp.asarray(exp_m, BF16)


def _ssd_pass(xbc, u_all, dtb_row, a_row, h0, direction, final_args=None):
    L = xbc.shape[0]
    nc = L // CHUNK
    reverse = direction == 1
    final = final_args is not None
    exp_m = _ssd_tables(direction)
    bc_w = SSM_GROUPS * SSM_STATE
    cidx = (lambda c: nc - 1 - c) if reverse else (lambda c: c)
    st_shape = (SSM_GROUPS, SSM_HPG // 2, SSM_STATE, LANES)
    in_specs = [
        pl.BlockSpec((CHUNK, BRANCH_W), lambda c: (cidx(c), 0)),
        pl.BlockSpec((CHUNK, bc_w), lambda c: (cidx(c), BRANCH_W // bc_w)),
        pl.BlockSpec((CHUNK, bc_w), lambda c: (cidx(c), BRANCH_W // bc_w + 1)),
        pl.BlockSpec((CHUNK, LANES), lambda c: (cidx(c), OFF_DT // LANES)),
        pl.BlockSpec((1, LANES), lambda c: (0, 0)),
        pl.BlockSpec((1, LANES), lambda c: (0, 0)),
        pl.BlockSpec((SSM_GROUPS, LANES, SSM_HPG * LANES), lambda c: (0, 0, 0)),
        pl.BlockSpec(st_shape, lambda c: (0, 0, 0, 0)),
    ]
    args = [xbc, xbc, xbc, u_all, dtb_row, a_row, exp_m, h0]
    if final:
        y_prev, dsk_row, nrm_row = final_args
        in_specs += [
            pl.BlockSpec((CHUNK, BRANCH_W), lambda c: (cidx(c), 0)),
            pl.BlockSpec((CHUNK, BRANCH_W), lambda c: (cidx(c), OFF_Z // BRANCH_W)),
            pl.BlockSpec((1, BRANCH_W), lambda c: (0, 0)),
            pl.BlockSpec((1, BRANCH_W), lambda c: (0, 0)),
        ]
        args += [y_prev, u_all, dsk_row, nrm_row]
    out_dtype = BF16 if final else F32
    y, hfin = pl.pallas_call(
        functools.partial(_ssd_kernel, reverse=reverse, final=final, direction=direction),
        grid=(nc,),
        in_specs=in_specs,
        out_specs=[pl.BlockSpec((CHUNK, BRANCH_W), lambda c: (cidx(c), 0)),
                   pl.BlockSpec(st_shape, lambda c: (0, 0, 0, 0))],
        out_shape=[jax.ShapeDtypeStruct((L, BRANCH_W), out_dtype),
                   jax.ShapeDtypeStruct(st_shape, F32)],
        scratch_shapes=[pltpu.VMEM(st_shape, F32)],
        compiler_params=_cparams(("arbitrary",)),
        name="ssd_bwd" if reverse else "ssd_fwd",
    )(*args)
    return y, hfin


def _conv_kernel(u_ref, up_ref, un_ref, w_ref, b_ref, o_ref, *, tm):
    i = pl.program_id(0)
    halo = SUBLANES
    n = tm + 2 * halo
    cur = u_ref[...]
    prev = jnp.where(i > 0, up_ref[...], 0.0)
    nxt = jnp.where(i < pl.num_programs(0) - 1, un_ref[...], 0.0)
    ext = jnp.concatenate([prev, cur, nxt], axis=0)
    mid = CONV_W // 2
    acc = b_ref[...] + w_ref[mid:mid + 1, :] * cur
    for k in range(CONV_W):
        d = k - mid
        if d != 0:
            acc = acc + w_ref[k:k + 1, :] * pltpu.roll(ext, (-d) % n, 0)[halo:halo + tm]
    o_ref[...] = (acc * _sigmoid(acc)).astype(o_ref.dtype)


def _conv_silu(u_all, conv_w, conv_b, tm):
    L = u_all.shape[0]
    cw = SSM_GROUPS * SSM_STATE
    hb = tm // SUBLANES
    last = L // SUBLANES - 1
    c0 = OFF_XBC // cw
    return pl.pallas_call(
        functools.partial(_conv_kernel, tm=tm),
        grid=(L // tm, XBC_W // cw),
        in_specs=[pl.BlockSpec((tm, cw), lambda i, j: (i, c0 + j)),
                  pl.BlockSpec((SUBLANES, cw), lambda i, j: (jnp.maximum(i * hb - 1, 0), c0 + j)),
                  pl.BlockSpec((SUBLANES, cw), lambda i, j: (jnp.minimum((i + 1) * hb, last), c0 + j)),
                  pl.BlockSpec((CONV_W, cw), lambda i, j: (0, j)),
                  pl.BlockSpec((1, cw), lambda i, j: (0, j))],
        out_specs=pl.BlockSpec((tm, cw), lambda i, j: (i, j)),
        out_shape=jax.ShapeDtypeStruct((L, XBC_W), BF16),
        compiler_params=_cparams(("parallel", "parallel")),
        name="conv_silu",
    )(u_all, u_all, u_all, conv_w.astype(F32), conv_b.astype(F32)[None, :])


def _pad_row(v, width=LANES):
    return jnp.zeros((1, width), F32).at[0, :v.shape[0]].set(v.astype(F32))


def _mamba(u_all, conv_w, conv_b, dt_bias, a_log, d_skip, ssm_norm, h0_f, h0_b):
    xbc = _conv_silu(u_all, conv_w, conv_b, min(u_all.shape[0], 512))
    dtb_row = _pad_row(dt_bias.reshape(-1))
    a_row = _pad_row(-jnp.exp(a_log.astype(F32)).reshape(-1))
    dsk_row = jnp.repeat(d_skip.astype(F32), SSM_HEAD_DIM)[None, :]
    nrm_row = ssm_norm.astype(F32)[None, :]
    y_f, hf = _ssd_pass(xbc, u_all, dtb_row, a_row, h0_f, 0)
    y, hb = _ssd_pass(xbc, u_all, dtb_row, a_row, h0_b, 1, (y_f, dsk_row, nrm_row))
    return y, hf, hb


def _gla_chunk(uq, uf, v, lb, st_t, ebc, sel_small, bd_mask, reverse):
    Q = CHUNK
    rows = lax.broadcasted_iota(jnp.int32, (Q, LANES), 0)
    lanes = lax.broadcasted_iota(jnp.int32, (Q, LANES), 1)
    r2 = lax.broadcasted_iota(jnp.int32, (Q, Q), 0)
    c2 = lax.broadcasted_iota(jnp.int32, (Q, Q), 1)
    tri_bf = ((r2 <= c2) if reverse else (r2 >= c2)).astype(BF16)
    lane_lo = lanes < HGRN_KEY
    end_row = 0 if reverse else Q - 1

    q = _silu(uq)
    f = lb + (1.0 - lb) * _sigmoid(uf)
    logf = jnp.log(jnp.maximum(f, F_MIN))
    k = (1.0 - lb) * _sigmoid(-uf)
    gc = _dot01_l(tri_bf, logf)

    xr = r2 ^ c2
    att = jnp.dot((q * k).astype(BF16), ebc, preferred_element_type=F32)
    a0 = jnp.where(xr == 0, att[:, :HGRN_VAL], 0.0)
    a1 = jnp.where(xr == 0, att[:, HGRN_VAL:], 0.0)
    b = 1
    lvl = 0
    while b < Q:
        if 2 * b < SUBLANES:
            ref = _dot01_l(sel_small[lvl], gc)
        else:
            pieces = []
            for p0 in range(0, Q, 2 * b):
                row = p0 + b if reverse else p0 + b - 1
                pieces.append(jnp.broadcast_to(gc[row:row + 1, :], (2 * b, LANES)))
            ref = jnp.concatenate(pieces, axis=0) if len(pieces) > 1 else pieces[0]
        hi = (rows & b) != 0
        is_q = (~hi) if reverse else hi
        qt = (q * jnp.exp(jnp.where(is_q, gc - ref, NEG_BIG))).astype(BF16)
        ek = ref - gc
        kt0 = (k * jnp.exp(jnp.where((~is_q) & lane_lo, ek, NEG_BIG))).astype(BF16)
        kt1 = (k * jnp.exp(jnp.where((~is_q) & (~lane_lo), ek, NEG_BIG))).astype(BF16)
        att = _dot_nt(qt, jnp.concatenate([kt0, kt1], axis=0))
        if 2 * b < Q:
            same = xr < 2 * b
            a0 = jnp.where(same & (xr >= b), att[:, :Q], a0)
            a1 = jnp.where(same & (xr >= b), att[:, Q:], a1)
        else:
            a0 = jnp.where(xr >= b, att[:, :Q], a0)
            a1 = jnp.where(xr >= b, att[:, Q:], a1)
        b *= 2
        lvl += 1
    vb = v.astype(BF16)
    o = jnp.concatenate([jnp.dot(a0.astype(BF16), vb[:, :HGRN_VAL], preferred_element_type=F32),
                         jnp.dot(a1.astype(BF16), vb[:, HGRN_VAL:], preferred_element_type=F32)], axis=1)

    qg = (q * jnp.exp(gc)).astype(BF16)
    o = o + _dot_nt(qg, st_t.astype(BF16))
    end = gc[end_row:end_row + 1, :]
    kd = (k * jnp.exp(end - gc)).astype(BF16)
    st_new = st_t * jnp.exp(end) + jnp.dot(v.T.astype(BF16), kd, preferred_element_type=F32) * bd_mask
    return o, st_new


def _gla_kernel(u_hbm, lb_ref, ebc_ref, sel_ref, s0_ref, o_hbm, sfin_ref, st_ref, bq, bf, bv, bo, isem, osem,
                *, reverse, direction, ncol, nrow_chunks):
    p = pl.program_id(0)
    c = pl.program_id(1)
    col = (ncol - 1 - c) if reverse else c
    nxt = (col - 1) if reverse else (col + 1)
    slot = c % 2
    vw = 2 * HGRN_VAL
    q_off = pl.multiple_of(OFF_Q + p * LANES, LANES)
    f_off = pl.multiple_of(OFF_F + direction * HGRN_HK + p * LANES, LANES)
    v_off = pl.multiple_of(OFF_I + p * vw, LANES)
    o_off = pl.multiple_of(p * vw, LANES)

    def in_copies(col_, slot_):
        return (pltpu.make_async_copy(u_hbm.at[:, col_, pl.ds(q_off, LANES)], bq.at[slot_], isem.at[0, slot_]),
                pltpu.make_async_copy(u_hbm.at[:, col_, pl.ds(f_off, LANES)], bf.at[slot_], isem.at[1, slot_]),
                pltpu.make_async_copy(u_hbm.at[:, col_, pl.ds(v_off, vw)], bv.at[slot_], isem.at[2, slot_]))

    def out_copy(col_, slot_):
        return pltpu.make_async_copy(bo.at[slot_], o_hbm.at[:, col_, pl.ds(o_off, vw)], osem.at[slot_])

    @pl.when(c == 0)
    def _():
        st_ref[...] = s0_ref[...]
        for i, cp in enumerate(in_copies(col, slot)):
            cp.start(priority=i // 2)

    for cp in in_copies(col, slot):
        cp.wait()

    @pl.when(c + 1 < ncol)
    def _():
        for i, cp in enumerate(in_copies(nxt, 1 - slot)):
            cp.start(priority=i // 2)

    @pl.when(c >= 2)
    def _():
        out_copy(col, slot).wait()

    r2 = lax.broadcasted_iota(jnp.int32, (vw, LANES), 0)
    c2 = lax.broadcasted_iota(jnp.int32, (vw, LANES), 1)
    bd_mask = ((r2 < HGRN_VAL) == (c2 < HGRN_KEY)).astype(F32)
    lb = lb_ref[...]
    ebc = ebc_ref[...]
    sel_small = [sel_ref[i] for i in range(sel_ref.shape[0])]
    order = range(nrow_chunks - 1, -1, -1) if reverse else range(nrow_chunks)
    for rc in order:
        sl = pl.ds(rc * CHUNK, CHUNK)
        o, st_new = _gla_chunk(bq[slot, sl, :], bf[slot, sl, :], bv[slot, sl, :], lb, st_ref[...], ebc, sel_small,
                               bd_mask, reverse)
        st_ref[...] = st_new
        bo[slot, sl, :] = o

    out_copy(col, slot).start()

    @pl.when(c == ncol - 1)
    def _():
        out_copy(col, slot).wait()
        if ncol >= 2:
            out_copy(col, 1 - slot).wait()
        sfin_ref[...] = st_ref[...]


def _gla_pass(u3, lb_rows, s0, direction):
    R, Wc, _ = u3.shape
    reverse = direction == 1
    ebc = np.zeros((LANES, 2 * HGRN_VAL), np.float32)
    ebc[:HGRN_KEY, :HGRN_VAL] = 1.0
    ebc[HGRN_KEY:, HGRN_VAL:] = 1.0
    ebc = jnp.asarray(ebc, BF16)
    small = [b for b in (1, 2, 4) if 2 * b < SUBLANES]
    sel = np.zeros((len(small), CHUNK, CHUNK), np.float32)
    for i, b in enumerate(small):
        for r in range(CHUNK):
            p0 = (r // (2 * b)) * 2 * b
            sel[i, r, p0 + b if reverse else p0 + b - 1] = 1.0
    sel = jnp.asarray(sel, BF16)
    vw = 2 * HGRN_VAL
    o, sfin = pl.pallas_call(
        functools.partial(_gla_kernel, reverse=reverse, direction=direction, ncol=Wc, nrow_chunks=R // CHUNK),
        grid=(HGRN_PAIRS, Wc),
        in_specs=[pl.BlockSpec(memory_space=pl.ANY),
                  pl.BlockSpec((None, 1, LANES), lambda p, c: (p, 0, 0)),
                  pl.BlockSpec((LANES, vw), lambda p, c: (0, 0)),
                  pl.BlockSpec((len(small), CHUNK, CHUNK), lambda p, c: (0, 0, 0)),
                  pl.BlockSpec((None, vw, LANES), lambda p, c: (p, 0, 0))],
        out_specs=[pl.BlockSpec(memory_space=pl.ANY),
                   pl.BlockSpec((None, vw, LANES), lambda p, c: (p, 0, 0))],
        out_shape=[jax.ShapeDtypeStruct((R, Wc, BRANCH_W), F32),
                   jax.ShapeDtypeStruct((HGRN_PAIRS, vw, LANES), F32)],
        scratch_shapes=[pltpu.VMEM((vw, LANES), F32),
                        pltpu.VMEM((2, R, LANES), F32), pltpu.VMEM((2, R, LANES), F32),
                        pltpu.VMEM((2, R, vw), F32), pltpu.VMEM((2, R, vw), F32),
                        pltpu.SemaphoreType.DMA((3, 2)), pltpu.SemaphoreType.DMA((2,))],
        compiler_params=_cparams(("arbitrary", "arbitrary")),
        name="gla_bwd" if reverse else "gla_fwd",
    )(u3, lb_rows, ebc, sel, s0)
    return o, sfin


def _hgrn_post_kernel(of_ref, ob_ref, g_ref, nrm_ref, y_ref):
    o = of_ref[...] + ob_ref[...]
    gate = _silu(g_ref[...])
    nrm = nrm_ref[...]
    for h in range(2):
        sl = slice(HGRN_VAL * h, HGRN_VAL * (h + 1))
        oh = o[:, sl]
        oh = oh * lax.rsqrt(jnp.mean(oh * oh, axis=-1, keepdims=True) + EPS)
        y_ref[:, sl] = (oh * nrm * gate[:, sl]).astype(y_ref.dtype)


def _hgrn_post(o_f, o_b, u_all, nrm_row, tm):
    L = o_f.shape[0]
    vw = 2 * HGRN_VAL
    blk = pl.BlockSpec((tm, vw), lambda i, p: (i, p))
    return pl.pallas_call(
        _hgrn_post_kernel,
        grid=(L // tm, HGRN_PAIRS),
        in_specs=[blk, blk, pl.BlockSpec((tm, vw), lambda i, p: (i, OFF_G // vw + p)),
                  pl.BlockSpec((1, HGRN_VAL), lambda i, p: (0, 0))],
        out_specs=blk,
        out_shape=jax.ShapeDtypeStruct((L, BRANCH_W), BF16),
        compiler_params=_cparams(("parallel", "parallel")),
        name="hgrn_post",
    )(o_f, o_b, u_all, nrm_row)


def _hgrn(u_all, lower_bound, hgrn_norm, s0_f, s0_b, grid_w):
    L = u_all.shape[0]
    R = L // grid_w
    u3 = u_all.reshape(R, grid_w, N_PAD)
    lb = lower_bound.astype(F32).reshape(2, HGRN_PAIRS, 1, LANES)
    o_f, sf = _gla_pass(u3, lb[0], s0_f, 0)
    o_b, sb = _gla_pass(u3, lb[1], s0_b, 1)
    y = _hgrn_post(o_f.reshape(L, BRANCH_W), o_b.reshape(L, BRANCH_W), u_all, hgrn_norm.astype(F32)[None, :],
                   min(L, 1024))
    return y, sf, sb


def _dft_tables(R, Wc):
    L = R * Wc
    q = np.arange(R)[None, :, None]
    r = np.arange(R)[None, None, :]
    c = np.arange(Wc)[:, None, None]
    ang = 2.0 * np.pi * ((q * (Wc * r + c)) % L) / L
    m = np.concatenate([np.cos(ang), -np.sin(ang)], axis=1)
    qc = np.arange(Wc)[:, None]
    cc = np.arange(Wc)[None, :]
    a2 = 2.0 * np.pi * ((qc * cc) % Wc) / Wc
    t = np.block([[np.cos(a2), np.sin(a2)], [np.sin(a2), -np.cos(a2)]])
    ch = np.arange(FOURIER_GW)
    a3 = 2.0 * np.pi * ((ch[:, None] * ch[None, :]) % FOURIER_GW) / FOURIER_GW
    cs = np.concatenate([np.cos(a3), -np.sin(a3)], axis=0) / math.sqrt(L * FOURIER_GW)
    return (jnp.asarray(m, F32).astype(BF16), jnp.asarray(t, F32).astype(BF16), jnp.asarray(cs, F32).astype(BF16))


def _fnet_a_kernel(x_ref, m_ref, y_ref, *, wb, R):
    for jj in range(wb):
        res = jnp.dot(m_ref[jj], x_ref[:, jj, :].astype(BF16), preferred_element_type=F32)
        y_ref[:, 0, jj, :] = res[:R]
        y_ref[:, 1, jj, :] = res[R:]


def _fnet_b_kernel(y_ref, t_ref, cs_ref, o_ref, *, Wc, rb):
    t = t_ref[...]
    cs = cs_ref[...]
    for i in range(rb):
        yy = y_ref[i].reshape(2 * Wc, FOURIER_GW).astype(BF16)
        pq = jnp.dot(t, yy, preferred_element_type=F32)
        lhs = jnp.concatenate([pq[:Wc], pq[Wc:]], axis=1).astype(BF16)
        o_ref[:, i, :] = jnp.dot(lhs, cs, preferred_element_type=F32).astype(o_ref.dtype)


def _fnet(u_all, grid_w):
    L = u_all.shape[0]
    R = L // grid_w
    Wc = grid_w
    wb = SUBLANES
    rb = SUBLANES
    m, t, cs = _dft_tables(R, Wc)
    u3 = u_all.reshape(R, Wc, N_PAD)
    y = pl.pallas_call(
        functools.partial(_fnet_a_kernel, wb=wb, R=R),
        grid=(Wc // wb, FOURIER_GROUPS),
        in_specs=[pl.BlockSpec((R, wb, FOURIER_GW), lambda cb, g: (0, cb, OFF_FOUR // FOURIER_GW + g)),
                  pl.BlockSpec((wb, 2 * R, R), lambda cb, g: (cb, 0, 0))],
        out_specs=pl.BlockSpec((R, 2, wb, FOURIER_GW), lambda cb, g: (0, 0, cb, g)),
        out_shape=jax.ShapeDtypeStruct((R, 2, Wc, BRANCH_W), F32),
        compiler_params=_cparams(("parallel", "arbitrary")),
        name="fnet_a",
    )(u3, m)
    out = pl.pallas_call(
        functools.partial(_fnet_b_kernel, Wc=Wc, rb=rb),
        grid=(R // rb, FOURIER_GROUPS),
        in_specs=[pl.BlockSpec((rb, 2, Wc, FOURIER_GW), lambda qb, g: (qb, 0, 0, g)),
                  pl.BlockSpec((2 * Wc, 2 * Wc), lambda qb, g: (0, 0)),
                  pl.BlockSpec((2 * FOURIER_GW, FOURIER_GW), lambda qb, g: (0, 0))],
        out_specs=pl.BlockSpec((Wc, rb, FOURIER_GW), lambda qb, g: (0, qb, g)),
        out_shape=jax.ShapeDtypeStruct((Wc, R, BRANCH_W), BF16),
        compiler_params=_cparams(("parallel", "arbitrary")),
        name="fnet_b",
    )(y, t, cs)
    return out.reshape(L, BRANCH_W)


def _fnet_small_kernel(x_ref, m_ref, cs_ref, o_ref, *, L):
    pq = jnp.dot(m_ref[...], x_ref[...].astype(BF16), preferred_element_type=F32)
    lhs = jnp.concatenate([pq[:L], -pq[L:]], axis=1).astype(BF16)
    o_ref[...] = jnp.dot(lhs, cs_ref[...], preferred_element_type=F32).astype(o_ref.dtype)


def _fnet_small(u_all):
    L = u_all.shape[0]
    m, _, cs = _dft_tables(L, 1)
    return pl.pallas_call(
        functools.partial(_fnet_small_kernel, L=L),
        grid=(FOURIER_GROUPS,),
        in_specs=[pl.BlockSpec((L, FOURIER_GW), lambda g: (0, OFF_FOUR // FOURIER_GW + g)),
                  pl.BlockSpec((2 * L, L), lambda g: (0, 0)),
                  pl.BlockSpec((2 * FOURIER_GW, FOURIER_GW), lambda g: (0, 0))],
        out_specs=pl.BlockSpec((L, FOURIER_GW), lambda g: (0, g)),
        out_shape=jax.ShapeDtypeStruct((L, BRANCH_W), BF16),
        compiler_params=_cparams(("parallel",)),
        name="fnet_small",
    )(u_all, m[0], cs)


def _merge_kernel(ym_ref, yh_ref, yf_ref, ug_ref, gu_ref, gb_ref, wbr_ref, wo_ref, x_ref, g1_ref, gt_ref,
                  o_ref, mg_ref, acc_ref, ss_ref, *, nb, tn):
    j = pl.program_id(1)

    @pl.when(j < nb)
    def _():
        ug = ug_ref[...].astype(BF16)
        tot = None
        for i, y_ref in enumerate((ym_ref, yh_ref, yf_ref)):
            gate = _sigmoid(jnp.dot(ug, gu_ref[i], preferred_element_type=F32) + gb_ref[i])
            term = gate * jnp.dot(y_ref[...], wbr_ref[i], preferred_element_type=F32)
            tot = term if tot is None else tot + term
        mg_ref[j] = tot.astype(BF16)

    @pl.when((j >= nb) & (j < 2 * nb))
    def _():
        acc = jnp.dot(mg_ref[0], wo_ref[0:tn, :], preferred_element_type=F32)
        for kk in range(1, nb):
            acc = acc + jnp.dot(mg_ref[kk], wo_ref[kk * tn:(kk + 1) * tn, :], preferred_element_type=F32)
        acc_ref[j - nb] = acc
        part = jnp.sum(acc * acc, axis=-1, keepdims=True)

        @pl.when(j == nb)
        def _():
            ss_ref[...] = part

        @pl.when(j > nb)
        def _():
            ss_ref[...] = ss_ref[...] + part

    @pl.when(j >= 2 * nb)
    def _():
        inv = lax.rsqrt(ss_ref[...] * (1.0 / (nb * tn)) + EPS)
        o_ref[...] = x_ref[...] + gt_ref[...] * ((acc_ref[j - 2 * nb] * inv) * g1_ref[...])


def _merge(ym, yh, yf, u_all, gate_up, gate_b, w_branch, w_out, layer, x, g1, gate, tm, tn):
    m, d = x.shape
    nb = d // tn
    p1 = lambda i, j: jnp.minimum(j, nb - 1)
    p2 = lambda i, j: jnp.clip(j - nb, 0, nb - 1)
    p3 = lambda i, j: jnp.clip(j - 2 * nb, 0, nb - 1)
    ybs = pl.BlockSpec((tm, BRANCH_W), lambda i, j: (i, 0))
    return pl.pallas_call(
        functools.partial(_merge_kernel, nb=nb, tn=tn),
        grid=(m // tm, 3 * nb),
        in_specs=[ybs, ybs, ybs,
                  pl.BlockSpec((tm, GATE_RANK), lambda i, j: (i, OFF_GATE // GATE_RANK)),
                  pl.BlockSpec((None, 3, GATE_RANK, tn), lambda i, j: (layer, 0, 0, p1(i, j))),
                  pl.BlockSpec((None, 3, 1, tn), lambda i, j: (layer, 0, 0, p1(i, j))),
                  pl.BlockSpec((None, 3, BRANCH_W, tn), lambda i, j: (layer, 0, 0, p1(i, j))),
                  pl.BlockSpec((None, d, tn), lambda i, j: (layer, 0, p2(i, j))),
                  pl.BlockSpec((tm, tn), lambda i, j: (i, p3(i, j))),
                  pl.BlockSpec((1, tn), lambda i, j: (0, p3(i, j))),
                  pl.BlockSpec((1, tn), lambda i, j: (0, p3(i, j)))],
        out_specs=pl.BlockSpec((tm, tn), lambda i, j: (i, p3(i, j))),
        out_shape=jax.ShapeDtypeStruct((m, d), F32),
        scratch_shapes=[pltpu.VMEM((nb, tm, tn), BF16), pltpu.VMEM((nb, tm, tn), F32), pltpu.VMEM((tm, 1), F32)],
        compiler_params=_cparams(("parallel", "arbitrary")),
        name="merge",
    )(ym, yh, yf, u_all, gate_up, gate_b, w_branch, w_out, x, g1, gate)


def _mlp_kernel(x_ref, g_ref, sh_ref, sc_ref, wu_ref, wd_ref, xb_ref, g3_ref, gt_ref, o_ref,
                hx_ref, h0_ref, h1_ref, acc_ref, inv_ref, *, nh, nb, tn):
    j = pl.program_id(1)

    def up(h_ref):
        h = jnp.dot(hx_ref[...], wu_ref[...], preferred_element_type=F32)
        h_ref[...] = jnp.square(jnp.maximum(h, 0.0)).astype(BF16)

    def down(h_ref):
        h = h_ref[...]
        for kk in range(nb):
            acc_ref[kk] = acc_ref[kk] + jnp.dot(h, wd_ref[:, kk * tn:(kk + 1) * tn], preferred_element_type=F32)

    @pl.when(j == 0)
    def _():
        hx_ref[...] = _rms_mod(x_ref[...], g_ref[...], sh_ref[...], sc_ref[...]).astype(BF16)
        acc_ref[...] = jnp.zeros(acc_ref.shape, F32)
        up(h0_ref)

    @pl.when((j > 0) & (j < nh) & (j % 2 == 1))
    def _():
        down(h0_ref)
        up(h1_ref)

    @pl.when((j > 0) & (j < nh) & (j % 2 == 0))
    def _():
        down(h1_ref)
        up(h0_ref)

    @pl.when(j == nh)
    def _():
        down(h1_ref if (nh - 1) % 2 == 1 else h0_ref)
        ss = jnp.sum(jnp.square(acc_ref[0]), axis=-1, keepdims=True)
        for kk in range(1, nb):
            ss = ss + jnp.sum(jnp.square(acc_ref[kk]), axis=-1, keepdims=True)
        inv_ref[...] = lax.rsqrt(ss * (1.0 / (nb * tn)) + EPS)

    @pl.when(j > nh)
    def _():
        o_ref[...] = xb_ref[...] + gt_ref[...] * ((acc_ref[j - nh - 1] * inv_ref[...]) * g3_ref[...])


def _mlp(x, g2, shift, scale, w_up, w_down, layer, g3, gate, tm, th, tn):
    m, d = x.shape
    dff = w_up.shape[2]
    nh = dff // th
    nb = d // tn
    row = lambda i, j: (0, 0)
    ph = lambda i, j: jnp.minimum(j, nh - 1)
    pd = lambda i, j: jnp.clip(j - 1, 0, nh - 1)
    pe = lambda i, j: jnp.clip(j - nh - 1, 0, nb - 1)
    return pl.pallas_call(
        functools.partial(_mlp_kernel, nh=nh, nb=nb, tn=tn),
        grid=(m // tm, nh + 1 + nb),
        in_specs=[pl.BlockSpec((tm, d), lambda i, j: (i, 0), pipeline_mode=pl.Buffered(1)),
                  pl.BlockSpec((1, d), row), pl.BlockSpec((1, d), row), pl.BlockSpec((1, d), row),
                  pl.BlockSpec((None, d, th), lambda i, j: (layer, 0, ph(i, j))),
                  pl.BlockSpec((None, th, d), lambda i, j: (layer, pd(i, j), 0)),
                  pl.BlockSpec((tm, tn), lambda i, j: (i, pe(i, j))),
                  pl.BlockSpec((1, tn), lambda i, j: (0, pe(i, j))),
                  pl.BlockSpec((1, tn), lambda i, j: (0, pe(i, j)))],
        out_specs=pl.BlockSpec((tm, tn), lambda i, j: (i, pe(i, j))),
        out_shape=jax.ShapeDtypeStruct((m, d), F32),
        scratch_shapes=[pltpu.VMEM((tm, d), BF16), pltpu.VMEM((tm, th), BF16), pltpu.VMEM((tm, th), BF16),
                        pltpu.VMEM((nb, tm, tn), F32), pltpu.VMEM((tm, 1), F32)],
        compiler_params=_cparams(("parallel", "arbitrary")),
        name="mlp",
    )(x, g2, shift, scale, w_up, w_down, x, g3, gate)


_DT_SHIFT = IN_SPLITS[2]


def _wprep_table():
    src, mode = [], []
    nblk = lambda w: w // LANES
    add = lambda a0, n, m: (src.extend(range(a0, a0 + n)), mode.extend([m] * n))
    add(0, nblk(BRANCH_W), 1)
    add((sum(IN_SPLITS[:7]) - _DT_SHIFT) // LANES, nblk(BRANCH_W), 0)
    add(BRANCH_W // LANES, nblk(XBC_W), 1)
    add((BRANCH_W + XBC_W) // LANES, nblk(HGRN_HK + 2 * HGRN_HK + 2 * BRANCH_W), 0)
    add((sum(IN_SPLITS[:8]) - _DT_SHIFT) // LANES, nblk(GATE_RANK), 0)
    add((BRANCH_W + XBC_W) // LANES, 1, 2)
    while len(src) < N_PAD // LANES:
        add(0, 1, 3)
    return np.asarray(src, np.int32), np.asarray(mode, np.int32)


def _wprep_kernel(src_ref, mode_ref, a_ref, b_ref, o_ref):
    mode = mode_ref[pl.program_id(0)]
    lanes = lax.broadcasted_iota(jnp.int32, a_ref.shape, 1)

    @pl.when(mode == 0)
    def _():
        ra = pltpu.roll(a_ref[...], LANES - _DT_SHIFT, 1)
        rb = pltpu.roll(b_ref[...], LANES - _DT_SHIFT, 1)
        o_ref[...] = jnp.where(lanes < LANES - _DT_SHIFT, ra, rb).astype(o_ref.dtype)

    @pl.when(mode == 1)
    def _():
        o_ref[...] = a_ref[...].astype(o_ref.dtype)

    @pl.when(mode == 2)
    def _():
        o_ref[...] = jnp.where(lanes < _DT_SHIFT, a_ref[...], 0.0).astype(o_ref.dtype)

    @pl.when(mode == 3)
    def _():
        o_ref[...] = jnp.zeros(o_ref.shape, o_ref.dtype)


def _permute_w_in(w, layer):
    _, d, n = w.shape
    src, mode = _wprep_table()
    last = (n - 1) // LANES
    return pl.pallas_call(
        _wprep_kernel,
        grid_spec=pltpu.PrefetchScalarGridSpec(
            num_scalar_prefetch=2,
            grid=(N_PAD // LANES,),
            in_specs=[pl.BlockSpec((None, d, LANES), lambda j, src, mode: (layer, 0, src[j])),
                      pl.BlockSpec((None, d, LANES), lambda j, src, mode: (layer, 0, jnp.minimum(src[j] + 1, last)))],
            out_specs=pl.BlockSpec((d, LANES), lambda j, src, mode: (0, j))),
        out_shape=jax.ShapeDtypeStruct((d, N_PAD), BF16),
        compiler_params=_cparams(("arbitrary",)),
        name="wprep",
    )(jnp.asarray(src), jnp.asarray(mode), w, w)


def _state_zeros():
    hm0 = jnp.zeros((SSM_GROUPS, SSM_HPG // 2, SSM_STATE, LANES), F32)
    sh0 = jnp.zeros((HGRN_PAIRS, 2 * HGRN_VAL, LANES), F32)
    return hm0, sh0


def kernel(x, c, ctx, c_ctx, ada_down, ada_up, ada_b, norm_g, w_in, conv_w, conv_b, ssm_dt_bias, ssm_a_log,
           ssm_d, ssm_norm, hgrn_lb_logits, hgrn_norm, gate_up, gate_b, w_branch, w_out, mlp_up, mlp_down):
    assert x.shape[0] == 1 and ctx.shape[0] == 1
    xs = x[0]
    cs = ctx[0]
    L, d = xs.shape
    lc = cs.shape[0]
    tm_x, tm_c = 512, lc
    p_lb = jax.nn.softmax(hgrn_lb_logits.astype(F32), axis=0)
    lower_bounds = jnp.clip(jnp.cumsum(p_lb, axis=0) - p_lb[0], 0.0, 1.0)
    cond = jnp.zeros((SUBLANES, d), F32).at[0].set(c[0]).at[1].set(c_ctx)
    cond = cond * jax.nn.sigmoid(cond)
    hm0, sh0 = _state_zeros()
    zero_b = jnp.zeros((1, ada_down.shape[-1]), F32)
    gu_all, gb_all = gate_up.astype(BF16), gate_b.astype(F32)[:, :, None, :]
    wbr_all, wo_all = w_branch.astype(BF16), w_out.astype(BF16)
    wu_all, wd_all = mlp_up.astype(BF16), mlp_down.astype(BF16)
    for l in range(DEPTH):
        last = l == DEPTH - 1
        g = norm_g[l].astype(F32)
        mods = _mm_small(_mm_small(cond, ada_down, zero_b, ada_down.shape[-1], l), ada_up, ada_b[l][None, :], 2048, l)
        sx1, cx1, gx1, sx2, cx2, gx2 = [mods[0:1, i * d:(i + 1) * d] for i in range(6)]
        sc1, cc1, gc1, sc2, cc2, gc2 = [mods[1:2, i * d:(i + 1) * d] for i in range(6)]
        w_in_l = _permute_w_in(w_in, l)
        mamba_p = (conv_w[l], conv_b[l], ssm_dt_bias[l], ssm_a_log[l], ssm_d[l], ssm_norm[l])

        uc = _inproj(cs, g[0:1], sc1, cc1, w_in_l, tm_c, 768)
        ym_c, hm_f, hm_b = _mamba(uc, *mamba_p, hm0, hm0)
        yh_c, sh_f, sh_b = _hgrn(uc, lower_bounds[l], hgrn_norm[l], sh0, sh0, 1)
        ux = _inproj(xs, g[0:1], sx1, cx1, w_in_l, tm_x, 768)
        ym, _, _ = _mamba(ux, *mamba_p, hm_f, hm_b)
        yh, _, _ = _hgrn(ux, lower_bounds[l], hgrn_norm[l], sh_f, sh_b, GRID_W)
        yf = _fnet(ux, GRID_W)
        xs = _merge(ym, yh, yf, ux, gu_all, gb_all, wbr_all, wo_all, l, xs, g[1:2], gx1, tm_x, 512)
        xs = _mlp(xs, g[2:3], sx2, cx2, wu_all, wd_all, l, g[3:4], gx2, tm_x, 512, 512)
        if not last:
            yf_c = _fnet_small(uc)
            cs = _merge(ym_c, yh_c, yf_c, uc, gu_all, gb_all, wbr_all, wo_all, l, cs, g[1:2], gc1, tm_c, 512)
            cs = _mlp(cs, g[2:3], sc2, cc2, wu_all, wd_all, l, g[3:4], gc2, tm_c, 512, 512)
    return xs[None]
```

```python
import functools
import math

import numpy as np
import jax
import jax.numpy as jnp
from jax import lax
from jax.experimental import pallas as pl
from jax.experimental.pallas import tpu as pltpu

F32 = jnp.float32
BF16 = jnp.bfloat16

D_MODEL = 4096
DEPTH = 4
GRID_W = 64
BRANCH_W = 1536
SSM_HEAD_DIM = 64
SSM_HEADS = 24
SSM_GROUPS = 4
SSM_HPG = 6
SSM_STATE = 128
CONV_W = 5
XBC_W = BRANCH_W + 2 * SSM_GROUPS * SSM_STATE
HGRN_VAL = 128
HGRN_HEADS = 12
HGRN_KEY = 64
HGRN_HK = HGRN_HEADS * HGRN_KEY
HGRN_PAIRS = HGRN_HEADS // 2
FOURIER_GROUPS = 4
FOURIER_GW = BRANCH_W // FOURIER_GROUPS
GATE_RANK = 256
D_FF = 4 * D_MODEL
CHUNK = 128
EPS = 1e-6
F_MIN = 1e-30
IN_SPLITS = (BRANCH_W, XBC_W, 2 * SSM_HEADS, HGRN_HK, 2 * HGRN_HK, BRANCH_W, BRANCH_W, BRANCH_W, GATE_RANK)

LANES = 128
SUBLANES = 8
VMEM_LIMIT = 56 * 1024 * 1024

OFF_Z = 0
OFF_FOUR = OFF_Z + BRANCH_W
OFF_XBC = OFF_FOUR + BRANCH_W
OFF_Q = OFF_XBC + XBC_W
OFF_F = OFF_Q + HGRN_HK
OFF_I = OFF_F + 2 * HGRN_HK
OFF_G = OFF_I + BRANCH_W
OFF_GATE = OFF_G + BRANCH_W
OFF_DT = OFF_GATE + GATE_RANK
N_PAD = 11520
NEG_BIG = -1e30
GLA_PP = 2


def _cparams(sem):
    return pltpu.CompilerParams(dimension_semantics=sem, vmem_limit_bytes=VMEM_LIMIT)


def _sigmoid(x):
    return 1.0 / (1.0 + jnp.exp(-x))


def _silu(x):
    return x * _sigmoid(x)


def _softplus(x):
    return jnp.maximum(x, 0.0) + jnp.log1p(jnp.exp(-jnp.abs(x)))


def _split3(x):
    hi = x.astype(BF16)
    r1 = x - hi.astype(F32)
    mid = r1.astype(BF16)
    lo = (r1 - mid.astype(F32)).astype(BF16)
    return hi, mid, lo


def _split2(x):
    hi = x.astype(BF16)
    return hi, (x - hi.astype(F32)).astype(BF16)


def _dot01_l(m01, x):
    hi, mid, lo = _split3(x)
    d = lambda a: jnp.dot(m01, a, preferred_element_type=F32)
    return d(hi) + d(mid) + d(lo)


def _dot_nt(a, b):
    return lax.dot_general(a, b, (((1,), (1,)), ((), ())), preferred_element_type=F32)


def _rms_mod(x, g, shift, scale):
    ms = jnp.mean(x * x, axis=-1, keepdims=True)
    y = (x * lax.rsqrt(ms + EPS)) * g
    return y * (1.0 + scale) + shift


def _mm_small_kernel(a_ref, w_ref, b_ref, o_ref):
    o_ref[...] = jnp.dot(a_ref[...].astype(BF16), w_ref[...].astype(BF16),
                         preferred_element_type=F32) + b_ref[...]


def _mm_small(a, w, b, tn, layer):
    m, k = a.shape
    n = w.shape[2]
    return pl.pallas_call(
        _mm_small_kernel,
        grid=(n // tn,),
        in_specs=[pl.BlockSpec((m, k), lambda j: (0, 0)),
                  pl.BlockSpec((None, k, tn), lambda j: (layer, 0, j)),
                  pl.BlockSpec((1, tn), lambda j: (0, j))],
        out_specs=pl.BlockSpec((m, tn), lambda j: (0, j)),
        out_shape=jax.ShapeDtypeStruct((m, n), F32),
        compiler_params=_cparams(("parallel",)),
        name="mm_small",
    )(a, w, b)


def _inproj_kernel(x_ref, g_ref, sh_ref, sc_ref, w_ref, o_ref, hx_ref):
    @pl.when(pl.program_id(1) == 0)
    def _():
        hx_ref[...] = _rms_mod(x_ref[...], g_ref[...], sh_ref[...], sc_ref[...]).astype(BF16)

    o_ref[...] = jnp.dot(hx_ref[...], w_ref[...], preferred_element_type=F32).astype(o_ref.dtype)


def _inproj(x, g, shift, scale, w, tm, tn):
    m, d = x.shape
    n = w.shape[1]
    row = lambda i, j: (0, 0)
    return pl.pallas_call(
        _inproj_kernel,
        grid=(m // tm, n // tn),
        in_specs=[pl.BlockSpec((tm, d), lambda i, j: (i, 0)),
                  pl.BlockSpec((1, d), row), pl.BlockSpec((1, d), row), pl.BlockSpec((1, d), row),
                  pl.BlockSpec((d, tn), lambda i, j: (0, j))],
        out_specs=pl.BlockSpec((tm, tn), lambda i, j: (i, j)),
        out_shape=jax.ShapeDtypeStruct((m, n), F32),
        scratch_shapes=[pltpu.VMEM((tm, d), BF16)],
        compiler_params=_cparams(("parallel", "arbitrary")),
        name="inproj",
    )(x, g, shift, scale, w)


def _ssd_kernel(*refs, reverse, final, direction):
    if final:
        (xs_ref, b_ref, c_ref, dt_ref, dtb_ref, a_ref, exp_ref, h0_ref,
         yp_ref, z_ref, dsk_ref, nrm_ref, y_ref, hfin_ref, st_ref) = refs
    else:
        (xs_ref, b_ref, c_ref, dt_ref, dtb_ref, a_ref, exp_ref, h0_ref, y_ref, hfin_ref, st_ref) = refs
    c = pl.program_id(0)
    nc = pl.num_programs(0)
    gw = BRANCH_W // SSM_GROUPS

    @pl.when(c == 0)
    def _():
        st_ref[...] = h0_ref[...]

    rows = lax.broadcasted_iota(jnp.int32, (CHUNK, CHUNK), 0)
    cols = lax.broadcasted_iota(jnp.int32, (CHUNK, CHUNK), 1)
    tri = (rows <= cols) if reverse else (rows >= cols)
    tri_bf = tri.astype(BF16)
    lane_lo = cols < SSM_HEAD_DIM
    end_row = 0 if reverse else CHUNK - 1

    dt_all = _softplus(dt_ref[...] + dtb_ref[...])
    dta = dt_all * a_ref[...]
    acum = _dot01_l(tri_bf, dta)
    acum_t = acum.T
    a_hi, a_lo = _split2(acum)
    dt_bf = dt_all.astype(BF16)

    for g in range(SSM_GROUPS):
        exp_m = exp_ref[g]
        acum_b = (jnp.dot(a_hi, exp_m, preferred_element_type=F32)
                  + jnp.dot(a_lo, exp_m, preferred_element_type=F32))
        dt_b = jnp.dot(dt_bf, exp_m, preferred_element_type=F32)
        xs = xs_ref[:, gw * g:gw * (g + 1)].astype(F32)
        bm = b_ref[:, SSM_STATE * g:SSM_STATE * (g + 1)]
        cm = c_ref[:, SSM_STATE * g:SSM_STATE * (g + 1)]
        bm_t = bm.astype(F32).T.astype(BF16)
        scores = _dot_nt(cm, bm)
        ys = []
        for j in range(SSM_HPG // 2):
            r0, r1 = 2 * j, 2 * j + 1
            col0 = direction * SSM_HEADS + g * SSM_HPG + r0
            ab0 = acum_b[:, LANES * r0:LANES * (r0 + 1)]
            ab1 = acum_b[:, LANES * r1:LANES * (r1 + 1)]
            acum_e = jnp.where(lane_lo, ab0, ab1)
            dt_e = jnp.where(lane_lo, dt_b[:, LANES * r0:LANES * (r0 + 1)], dt_b[:, LANES * r1:LANES * (r1 + 1)])
            xp = xs[:, LANES * j:LANES * (j + 1)] * dt_e
            end = acum_e[end_row:end_row + 1, :]
            xw = (xp * jnp.exp(end - acum_e)).astype(BF16)
            l0 = jnp.exp(jnp.where(tri, ab0 - acum_t[col0:col0 + 1, :], NEG_BIG))
            l1 = jnp.exp(jnp.where(tri, ab1 - acum_t[col0 + 1:col0 + 2, :], NEG_BIG))
            m0 = (scores * l0).astype(BF16)
            m1 = (scores * l1).astype(BF16)
            xp0 = jnp.where(lane_lo, xp, 0.0).astype(BF16)
            xp1 = jnp.where(lane_lo, 0.0, xp).astype(BF16)
            y_diag = (jnp.dot(m0, xp0, preferred_element_type=F32)
                      + jnp.dot(m1, xp1, preferred_element_type=F32))
            st = st_ref[g, j]
            y_off = jnp.dot(cm, st.astype(BF16), preferred_element_type=F32) * jnp.exp(acum_e)
            ys.append(y_diag + y_off)
            st_ref[g, j] = st * jnp.exp(end) + jnp.dot(bm_t, xw, preferred_element_type=F32)
        y = jnp.concatenate(ys, axis=1)
        if final:
            gs = slice(gw * g, gw * (g + 1))
            y = y + yp_ref[:, gs] + xs * dsk_ref[:, gs]
            y = y * _silu(z_ref[:, gs])
            y = y * lax.rsqrt(jnp.mean(y * y, axis=-1, keepdims=True) + EPS)
            y = y * nrm_ref[:, gs]
        y_ref[:, gw * g:gw * (g + 1)] = y.astype(y_ref.dtype)

    @pl.when(c == nc - 1)
    def _():
        hfin_ref[...] = st_ref[...]


def _ssd_tables(direction):
    exp_m = np.zeros((SSM_GROUPS, LANES, SSM_HPG * LANES), np.float32)
    for g in range(SSM_GROUPS):
        for r in range(SSM_HPG):
            col = direction * SSM_HEADS + g * SSM_HPG + r
            exp_m[g, col, LANES * r:LANES * (r + 1)] = 1.0
    return jnp.asarray(exp_m, BF16)


def _ssd_pass(xbc, u_all, dtb_row, a_row, h0, direction, final_args=None):
    L = xbc.shape[0]
    nc = L // CHUNK
    reverse = direction == 1
    final = final_args is not None
    exp_m = _ssd_tables(direction)
    bc_w = SSM_GROUPS * SSM_STATE
    cidx = (lambda c: nc - 1 - c) if reverse else (lambda c: c)
    st_shape = (SSM_GROUPS, SSM_HPG // 2, SSM_STATE, LANES)
    in_specs = [
        pl.BlockSpec((CHUNK, BRANCH_W), lambda c: (cidx(c), 0)),
        pl.BlockSpec((CHUNK, bc_w), lambda c: (cidx(c), BRANCH_W // bc_w)),
        pl.BlockSpec((CHUNK, bc_w), lambda c: (cidx(c), BRANCH_W // bc_w + 1)),
        pl.BlockSpec((CHUNK, LANES), lambda c: (cidx(c), OFF_DT // LANES)),
        pl.BlockSpec((1, LANES), lambda c: (0, 0)),
        pl.BlockSpec((1, LANES), lambda c: (0, 0)),
        pl.BlockSpec((SSM_GROUPS, LANES, SSM_HPG * LANES), lambda c: (0, 0, 0)),
        pl.BlockSpec(st_shape, lambda c: (0, 0, 0, 0)),
    ]
    args = [xbc, xbc, xbc, u_all, dtb_row, a_row, exp_m, h0]
    if final:
        y_prev, dsk_row, nrm_row = final_args
        in_specs += [
            pl.BlockSpec((CHUNK, BRANCH_W), lambda c: (cidx(c), 0)),
            pl.BlockSpec((CHUNK, BRANCH_W), lambda c: (cidx(c), OFF_Z // BRANCH_W)),
            pl.BlockSpec((1, BRANCH_W), lambda c: (0, 0)),
            pl.BlockSpec((1, BRANCH_W), lambda c: (0, 0)),
        ]
        args += [y_prev, u_all, dsk_row, nrm_row]
    out_dtype = BF16 if final else F32
    y, hfin = pl.pallas_call(
        functools.partial(_ssd_kernel, reverse=reverse, final=final, direction=direction),
        grid=(nc,),
        in_specs=in_specs,
        out_specs=[pl.BlockSpec((CHUNK, BRANCH_W), lambda c: (cidx(c), 0)),
                   pl.BlockSpec(st_shape, lambda c: (0, 0, 0, 0))],
        out_shape=[jax.ShapeDtypeStruct((L, BRANCH_W), out_dtype),
                   jax.ShapeDtypeStruct(st_shape, F32)],
        scratch_shapes=[pltpu.VMEM(st_shape, F32)],
        compiler_params=_cparams(("arbitrary",)),
        name="ssd_bwd" if reverse else "ssd_fwd",
    )(*args)
    return y, hfin


def _conv_kernel(u_ref, up_ref, un_ref, w_ref, b_ref, o_ref, *, tm):
    i = pl.program_id(0)
    halo = SUBLANES
    n = tm + 2 * halo
    cur = u_ref[...]
    prev = jnp.where(i > 0, up_ref[...], 0.0)
    nxt = jnp.where(i < pl.num_programs(0) - 1, un_ref[...], 0.0)
    ext = jnp.concatenate([prev, cur, nxt], axis=0)
    mid = CONV_W // 2
    acc = b_ref[...] + w_ref[mid:mid + 1, :] * cur
    for k in range(CONV_W):
        d = k - mid
        if d != 0:
            acc = acc + w_ref[k:k + 1, :] * pltpu.roll(ext, (-d) % n, 0)[halo:halo + tm]
    o_ref[...] = (acc * _sigmoid(acc)).astype(o_ref.dtype)


def _conv_silu(u_all, conv_w, conv_b, tm):
    L = u_all.shape[0]
    cw = SSM_GROUPS * SSM_STATE
    hb = tm // SUBLANES
    last = L // SUBLANES - 1
    c0 = OFF_XBC // cw
    return pl.pallas_call(
        functools.partial(_conv_kernel, tm=tm),
        grid=(L // tm, XBC_W // cw),
        in_specs=[pl.BlockSpec((tm, cw), lambda i, j: (i, c0 + j)),
                  pl.BlockSpec((SUBLANES, cw), lambda i, j: (jnp.maximum(i * hb - 1, 0), c0 + j)),
                  pl.BlockSpec((SUBLANES, cw), lambda i, j: (jnp.minimum((i + 1) * hb, last), c0 + j)),
                  pl.BlockSpec((CONV_W, cw), lambda i, j: (0, j)),
                  pl.BlockSpec((1, cw), lambda i, j: (0, j))],
        out_specs=pl.BlockSpec((tm, cw), lambda i, j: (i, j)),
        out_shape=jax.ShapeDtypeStruct((L, XBC_W), BF16),
        compiler_params=_cparams(("parallel", "parallel")),
        name="conv_silu",
    )(u_all, u_all, u_all, conv_w.astype(F32), conv_b.astype(F32)[None, :])


def _pad_row(v, width=LANES):
    return jnp.zeros((1, width), F32).at[0, :v.shape[0]].set(v.astype(F32))


def _mamba(u_all, conv_w, conv_b, dt_bias, a_log, d_skip, ssm_norm, h0_f, h0_b):
    xbc = _conv_silu(u_all, conv_w, conv_b, min(u_all.shape[0], 512))
    dtb_row = _pad_row(dt_bias.reshape(-1))
    a_row = _pad_row(-jnp.exp(a_log.astype(F32)).reshape(-1))
    dsk_row = jnp.repeat(d_skip.astype(F32), SSM_HEAD_DIM)[None, :]
    nrm_row = ssm_norm.astype(F32)[None, :]
    y_f, hf = _ssd_pass(xbc, u_all, dtb_row, a_row, h0_f, 0)
    y, hb = _ssd_pass(xbc, u_all, dtb_row, a_row, h0_b, 1, (y_f, dsk_row, nrm_row))
    return y, hf, hb


def _gla_chunk(uq, uf, v, lb, st_t, ebc, sel_small, bd_mask, reverse):
    Q = CHUNK
    rows = lax.broadcasted_iota(jnp.int32, (Q, LANES), 0)
    lanes = lax.broadcasted_iota(jnp.int32, (Q, LANES), 1)
    r2 = lax.broadcasted_iota(jnp.int32, (Q, Q), 0)
    c2 = lax.broadcasted_iota(jnp.int32, (Q, Q), 1)
    tri_bf = ((r2 <= c2) if reverse else (r2 >= c2)).astype(BF16)
    lane_lo = lanes < HGRN_KEY
    end_row = 0 if reverse else Q - 1

    q = _silu(uq)
    f = lb + (1.0 - lb) * _sigmoid(uf)
    logf = jnp.log(jnp.maximum(f, F_MIN))
    k = (1.0 - lb) * _sigmoid(-uf)
    gc = _dot01_l(tri_bf, logf)

    xr = r2 ^ c2
    att = jnp.dot((q * k).astype(BF16), ebc, preferred_element_type=F32)
    a0 = jnp.where(xr == 0, att[:, :HGRN_VAL], 0.0)
    a1 = jnp.where(xr == 0, att[:, HGRN_VAL:], 0.0)
    b = 1
    lvl = 0
    while b < Q:
        if 2 * b < SUBLANES:
            ref = _dot01_l(sel_small[lvl], gc)
        else:
            pieces = []
            for p0 in range(0, Q, 2 * b):
                row = p0 + b if reverse else p0 + b - 1
                pieces.append(jnp.broadcast_to(gc[row:row + 1, :], (2 * b, LANES)))
            ref = jnp.concatenate(pieces, axis=0) if len(pieces) > 1 else pieces[0]
        hi = (rows & b) != 0
        is_q = (~hi) if reverse else hi
        qt = (q * jnp.exp(jnp.where(is_q, gc - ref, NEG_BIG))).astype(BF16)
        ek = ref - gc
        kt0 = (k * jnp.exp(jnp.where((~is_q) & lane_lo, ek, NEG_BIG))).astype(BF16)
        kt1 = (k * jnp.exp(jnp.where((~is_q) & (~lane_lo), ek, NEG_BIG))).astype(BF16)
        att = _dot_nt(qt, jnp.concatenate([kt0, kt1], axis=0))
        if 2 * b < Q:
            same = xr < 2 * b
            a0 = jnp.where(same & (xr >= b), att[:, :Q], a0)
            a1 = jnp.where(same & (xr >= b), att[:, Q:], a1)
        else:
            a0 = jnp.where(xr >= b, att[:, :Q], a0)
            a1 = jnp.where(xr >= b, att[:, Q:], a1)
        b *= 2
        lvl += 1
    vb = v.astype(BF16)
    o = jnp.concatenate([jnp.dot(a0.astype(BF16), vb[:, :HGRN_VAL], preferred_element_type=F32),
                         jnp.dot(a1.astype(BF16), vb[:, HGRN_VAL:], preferred_element_type=F32)], axis=1)

    qg = (q * jnp.exp(gc)).astype(BF16)
    o = o + _dot_nt(qg, st_t.astype(BF16))
    end = gc[end_row:end_row + 1, :]
    kd = (k * jnp.exp(end - gc)).astype(BF16)
    st_new = st_t * jnp.exp(end) + jnp.dot(v.T.astype(BF16), kd, preferred_element_type=F32) * bd_mask
    return o, st_new


def _gla_kernel(u_hbm, lb_ref, ebc_ref, sel_ref, s0_ref, o_hbm, sfin_ref, st_ref, bq, bf, bv, bo, isem, osem,
                *, reverse, direction, ncol, nrow_chunks):
    p = pl.program_id(0)
    c = pl.program_id(1)
    col = (ncol - 1 - c) if reverse else c
    nxt = (col - 1) if reverse else (col + 1)
    slot = c % 2
    vw = 2 * HGRN_VAL
    kw2, vw2 = GLA_PP * LANES, GLA_PP * vw
    q_off = pl.multiple_of(OFF_Q + p * kw2, LANES)
    f_off = pl.multiple_of(OFF_F + direction * HGRN_HK + p * kw2, LANES)
    v_off = pl.multiple_of(OFF_I + p * vw2, LANES)
    o_off = pl.multiple_of(p * vw2, LANES)

    def in_copies(col_, slot_):
        return (pltpu.make_async_copy(u_hbm.at[:, col_, pl.ds(q_off, kw2)], bq.at[slot_], isem.at[0, slot_]),
                pltpu.make_async_copy(u_hbm.at[:, col_, pl.ds(f_off, kw2)], bf.at[slot_], isem.at[1, slot_]),
                pltpu.make_async_copy(u_hbm.at[:, col_, pl.ds(v_off, vw2)], bv.at[slot_], isem.at[2, slot_]))

    def out_copy(col_, slot_):
        return pltpu.make_async_copy(bo.at[slot_], o_hbm.at[:, col_, pl.ds(o_off, vw2)], osem.at[slot_])

    @pl.when(c == 0)
    def _():
        st_ref[...] = s0_ref[...]
        for i, cp in enumerate(in_copies(col, slot)):
            cp.start(priority=i // 2)

    for cp in in_copies(col, slot):
        cp.wait()

    @pl.when(c + 1 < ncol)
    def _():
        for i, cp in enumerate(in_copies(nxt, 1 - slot)):
            cp.start(priority=i // 2)

    @pl.when(c >= 2)
    def _():
        out_copy(col, slot).wait()

    r2 = lax.broadcasted_iota(jnp.int32, (vw, LANES), 0)
    c2 = lax.broadcasted_iota(jnp.int32, (vw, LANES), 1)
    bd_mask = ((r2 < HGRN_VAL) == (c2 < HGRN_KEY)).astype(F32)
    ebc = ebc_ref[...]
    sel_small = [sel_ref[i] for i in range(sel_ref.shape[0])]
    order = range(nrow_chunks - 1, -1, -1) if reverse else range(nrow_chunks)
    for rc in order:
        sl = pl.ds(rc * CHUNK, CHUNK)
        for pp in range(GLA_PP):
            ks = slice(LANES * pp, LANES * (pp + 1))
            vs = slice(vw * pp, vw * (pp + 1))
            o, st_new = _gla_chunk(bq[slot, sl, ks], bf[slot, sl, ks], bv[slot, sl, vs], lb_ref[:, ks], st_ref[pp],
                                   ebc, sel_small, bd_mask, reverse)
            st_ref[pp] = st_new
            bo[slot, sl, vs] = o

    out_copy(col, slot).start()

    @pl.when(c == ncol - 1)
    def _():
        out_copy(col, slot).wait()
        if ncol >= 2:
            out_copy(col, 1 - slot).wait()
        sfin_ref[...] = st_ref[...]


def _gla_pass(u3, lb_rows, s0, direction):
    R, Wc, _ = u3.shape
    reverse = direction == 1
    ebc = np.zeros((LANES, 2 * HGRN_VAL), np.float32)
    ebc[:HGRN_KEY, :HGRN_VAL] = 1.0
    ebc[HGRN_KEY:, HGRN_VAL:] = 1.0
    ebc = jnp.asarray(ebc, BF16)
    small = [b for b in (1, 2, 4) if 2 * b < SUBLANES]
    sel = np.zeros((len(small), CHUNK, CHUNK), np.float32)
    for i, b in enumerate(small):
        for r in range(CHUNK):
            p0 = (r // (2 * b)) * 2 * b
            sel[i, r, p0 + b if reverse else p0 + b - 1] = 1.0
    sel = jnp.asarray(sel, BF16)
    vw = 2 * HGRN_VAL
    o, sfin = pl.pallas_call(
        functools.partial(_gla_kernel, reverse=reverse, direction=direction, ncol=Wc, nrow_chunks=R // CHUNK),
        grid=(HGRN_PAIRS // GLA_PP, Wc),
        in_specs=[pl.BlockSpec(memory_space=pl.ANY),
                  pl.BlockSpec((None, 1, GLA_PP * LANES), lambda p, c: (p, 0, 0)),
                  pl.BlockSpec((LANES, vw), lambda p, c: (0, 0)),
                  pl.BlockSpec((len(small), CHUNK, CHUNK), lambda p, c: (0, 0, 0)),
                  pl.BlockSpec((GLA_PP, vw, LANES), lambda p, c: (p, 0, 0))],
        out_specs=[pl.BlockSpec(memory_space=pl.ANY),
                   pl.BlockSpec((GLA_PP, vw, LANES), lambda p, c: (p, 0, 0))],
        out_shape=[jax.ShapeDtypeStruct((R, Wc, BRANCH_W), F32),
                   jax.ShapeDtypeStruct((HGRN_PAIRS, vw, LANES), F32)],
        scratch_shapes=[pltpu.VMEM((GLA_PP, vw, LANES), F32),
                        pltpu.VMEM((2, R, GLA_PP * LANES), F32), pltpu.VMEM((2, R, GLA_PP * LANES), F32),
                        pltpu.VMEM((2, R, GLA_PP * vw), F32), pltpu.VMEM((2, R, GLA_PP * vw), F32),
                        pltpu.SemaphoreType.DMA((3, 2)), pltpu.SemaphoreType.DMA((2,))],
        compiler_params=_cparams(("arbitrary", "arbitrary")),
        name="gla_bwd" if reverse else "gla_fwd",
    )(u3, lb_rows, ebc, sel, s0)
    return o, sfin


def _hgrn_post_kernel(of_ref, ob_ref, g_ref, nrm_ref, y_ref):
    o = of_ref[...] + ob_ref[...]
    gate = _silu(g_ref[...])
    nrm = nrm_ref[...]
    for h in range(2):
        sl = slice(HGRN_VAL * h, HGRN_VAL * (h + 1))
        oh = o[:, sl]
        oh = oh * lax.rsqrt(jnp.mean(oh * oh, axis=-1, keepdims=True) + EPS)
        y_ref[:, sl] = (oh * nrm * gate[:, sl]).astype(y_ref.dtype)


def _hgrn_post(o_f, o_b, u_all, nrm_row, tm):
    L = o_f.shape[0]
    vw = 2 * HGRN_VAL
    blk = pl.BlockSpec((tm, vw), lambda i, p: (i, p))
    return pl.pallas_call(
        _hgrn_post_kernel,
        grid=(L // tm, HGRN_PAIRS),
        in_specs=[blk, blk, pl.BlockSpec((tm, vw), lambda i, p: (i, OFF_G // vw + p)),
                  pl.BlockSpec((1, HGRN_VAL), lambda i, p: (0, 0))],
        out_specs=blk,
        out_shape=jax.ShapeDtypeStruct((L, BRANCH_W), BF16),
        compiler_params=_cparams(("parallel", "parallel")),
        name="hgrn_post",
    )(o_f, o_b, u_all, nrm_row)


def _hgrn(u_all, lower_bound, hgrn_norm, s0_f, s0_b, grid_w):
    L = u_all.shape[0]
    R = L // grid_w
    u3 = u_all.reshape(R, grid_w, N_PAD)
    lb = lower_bound.astype(F32).reshape(2, HGRN_PAIRS // GLA_PP, 1, GLA_PP * LANES)
    o_f, sf = _gla_pass(u3, lb[0], s0_f, 0)
    o_b, sb = _gla_pass(u3, lb[1], s0_b, 1)
    y = _hgrn_post(o_f.reshape(L, BRANCH_W), o_b.reshape(L, BRANCH_W), u_all, hgrn_norm.astype(F32)[None, :],
                   min(L, 1024))
    return y, sf, sb


def _dft_tables(R, Wc):
    L = R * Wc
    q = np.arange(R)[None, :, None]
    r = np.arange(R)[None, None, :]
    c = np.arange(Wc)[:, None, None]
    ang = 2.0 * np.pi * ((q * (Wc * r + c)) % L) / L
    m = np.concatenate([np.cos(ang), -np.sin(ang)], axis=1)
    qc = np.arange(Wc)[:, None]
    cc = np.arange(Wc)[None, :]
    a2 = 2.0 * np.pi * ((qc * cc) % Wc) / Wc
    t = np.block([[np.cos(a2), np.sin(a2)], [np.sin(a2), -np.cos(a2)]])
    ch = np.arange(FOURIER_GW)
    a3 = 2.0 * np.pi * ((ch[:, None] * ch[None, :]) % FOURIER_GW) / FOURIER_GW
    cs = np.concatenate([np.cos(a3), -np.sin(a3)], axis=0) / math.sqrt(L * FOURIER_GW)
    return (jnp.asarray(m, F32).astype(BF16), jnp.asarray(t, F32).astype(BF16), jnp.asarray(cs, F32).astype(BF16))


def _fnet_a_kernel(x_ref, m_ref, y_ref, *, wb, R):
    for jj in range(wb):
        res = jnp.dot(m_ref[jj], x_ref[:, jj, :].astype(BF16), preferred_element_type=F32)
        y_ref[:, 0, jj, :] = res[:R]
        y_ref[:, 1, jj, :] = res[R:]


def _fnet_b_kernel(y_ref, t_ref, cs_ref, o_ref, *, Wc, rb):
    t = t_ref[...]
    cs = cs_ref[...]
    for i in range(rb):
        yy = y_ref[i].reshape(2 * Wc, FOURIER_GW).astype(BF16)
        pq = jnp.dot(t, yy, preferred_element_type=F32)
        lhs = jnp.concatenate([pq[:Wc], pq[Wc:]], axis=1).astype(BF16)
        o_ref[:, i, :] = jnp.dot(lhs, cs, preferred_element_type=F32).astype(o_ref.dtype)


def _fnet(u_all, grid_w):
    L = u_all.shape[0]
    R = L // grid_w
    Wc = grid_w
    wb = SUBLANES
    rb = SUBLANES
    m, t, cs = _dft_tables(R, Wc)
    u3 = u_all.reshape(R, Wc, N_PAD)
    y = pl.pallas_call(
        functools.partial(_fnet_a_kernel, wb=wb, R=R),
        grid=(Wc // wb, FOURIER_GROUPS),
        in_specs=[pl.BlockSpec((R, wb, FOURIER_GW), lambda cb, g: (0, cb, OFF_FOUR // FOURIER_GW + g)),
                  pl.BlockSpec((wb, 2 * R, R), lambda cb, g: (cb, 0, 0))],
        out_specs=pl.BlockSpec((R, 2, wb, FOURIER_GW), lambda cb, g: (0, 0, cb, g)),
        out_shape=jax.ShapeDtypeStruct((R, 2, Wc, BRANCH_W), F32),
        compiler_params=_cparams(("parallel", "arbitrary")),
        name="fnet_a",
    )(u3, m)
    out = pl.pallas_call(
        functools.partial(_fnet_b_kernel, Wc=Wc, rb=rb),
        grid=(R // rb, FOURIER_GROUPS),
        in_specs=[pl.BlockSpec((rb, 2, Wc, FOURIER_GW), lambda qb, g: (qb, 0, 0, g)),
                  pl.BlockSpec((2 * Wc, 2 * Wc), lambda qb, g: (0, 0)),
                  pl.BlockSpec((2 * FOURIER_GW, FOURIER_GW), lambda qb, g: (0, 0))],
        out_specs=pl.BlockSpec((Wc, rb, FOURIER_GW), lambda qb, g: (0, qb, g)),
        out_shape=jax.ShapeDtypeStruct((Wc, R, BRANCH_W), BF16),
        compiler_params=_cparams(("parallel", "arbitrary")),
        name="fnet_b",
    )(y, t, cs)
    return out.reshape(L, BRANCH_W)


def _fnet_small_kernel(x_ref, m_ref, cs_ref, o_ref, *, L):
    pq = jnp.dot(m_ref[...], x_ref[...].astype(BF16), preferred_element_type=F32)
    lhs = jnp.concatenate([pq[:L], -pq[L:]], axis=1).astype(BF16)
    o_ref[...] = jnp.dot(lhs, cs_ref[...], preferred_element_type=F32).astype(o_ref.dtype)


def _fnet_small(u_all):
    L = u_all.shape[0]
    m, _, cs = _dft_tables(L, 1)
    return pl.pallas_call(
        functools.partial(_fnet_small_kernel, L=L),
        grid=(FOURIER_GROUPS,),
        in_specs=[pl.BlockSpec((L, FOURIER_GW), lambda g: (0, OFF_FOUR // FOURIER_GW + g)),
                  pl.BlockSpec((2 * L, L), lambda g: (0, 0)),
                  pl.BlockSpec((2 * FOURIER_GW, FOURIER_GW), lambda g: (0, 0))],
        out_specs=pl.BlockSpec((L, FOURIER_GW), lambda g: (0, g)),
        out_shape=jax.ShapeDtypeStruct((L, BRANCH_W), BF16),
        compiler_params=_cparams(("parallel",)),
        name="fnet_small",
    )(u_all, m[0], cs)


def _merge_kernel(ym_ref, yh_ref, yf_ref, ug_ref, gu_ref, gb_ref, wbr_ref, wo_ref, x_ref, g1_ref, gt_ref,
                  o_ref, mg_ref, acc_ref, ss_ref, *, nb, tn):
    j = pl.program_id(1)

    @pl.when(j < nb)
    def _():
        ug = ug_ref[...].astype(BF16)
        tot = None
        for i, y_ref in enumerate((ym_ref, yh_ref, yf_ref)):
            gate = _sigmoid(jnp.dot(ug, gu_ref[i], preferred_element_type=F32) + gb_ref[i])
            term = gate * jnp.dot(y_ref[...], wbr_ref[i], preferred_element_type=F32)
            tot = term if tot is None else tot + term
        mg_ref[j] = tot.astype(BF16)

    @pl.when((j >= nb) & (j < 2 * nb))
    def _():
        acc = jnp.dot(mg_ref[0], wo_ref[0:tn, :], preferred_element_type=F32)
        for kk in range(1, nb):
            acc = acc + jnp.dot(mg_ref[kk], wo_ref[kk * tn:(kk + 1) * tn, :], preferred_element_type=F32)
        acc_ref[j - nb] = acc
        part = jnp.sum(acc * acc, axis=-1, keepdims=True)

        @pl.when(j == nb)
        def _():
            ss_ref[...] = part

        @pl.when(j > nb)
        def _():
            ss_ref[...] = ss_ref[...] + part

    @pl.when(j >= 2 * nb)
    def _():
        inv = lax.rsqrt(ss_ref[...] * (1.0 / (nb * tn)) + EPS)
        o_ref[...] = x_ref[...] + gt_ref[...] * ((acc_ref[j - 2 * nb] * inv) * g1_ref[...])


def _merge(ym, yh, yf, u_all, gate_up, gate_b, w_branch, w_out, layer, x, g1, gate, tm, tn):
    m, d = x.shape
    nb = d // tn
    p1 = lambda i, j: jnp.minimum(j, nb - 1)
    p2 = lambda i, j: jnp.clip(j - nb, 0, nb - 1)
    p3 = lambda i, j: jnp.clip(j - 2 * nb, 0, nb - 1)
    ybs = pl.BlockSpec((tm, BRANCH_W), lambda i, j: (i, 0))
    return pl.pallas_call(
        functools.partial(_merge_kernel, nb=nb, tn=tn),
        grid=(m // tm, 3 * nb),
        in_specs=[ybs, ybs, ybs,
                  pl.BlockSpec((tm, GATE_RANK), lambda i, j: (i, OFF_GATE // GATE_RANK)),
                  pl.BlockSpec((None, 3, GATE_RANK, tn), lambda i, j: (layer, 0, 0, p1(i, j))),
                  pl.BlockSpec((None, 3, 1, tn), lambda i, j: (layer, 0, 0, p1(i, j))),
                  pl.BlockSpec((None, 3, BRANCH_W, tn), lambda i, j: (layer, 0, 0, p1(i, j))),
                  pl.BlockSpec((None, d, tn), lambda i, j: (layer, 0, p2(i, j))),
                  pl.BlockSpec((tm, tn), lambda i, j: (i, p3(i, j))),
                  pl.BlockSpec((1, tn), lambda i, j: (0, p3(i, j))),
                  pl.BlockSpec((1, tn), lambda i, j: (0, p3(i, j)))],
        out_specs=pl.BlockSpec((tm, tn), lambda i, j: (i, p3(i, j))),
        out_shape=jax.ShapeDtypeStruct((m, d), F32),
        scratch_shapes=[pltpu.VMEM((nb, tm, tn), BF16), pltpu.VMEM((nb, tm, tn), F32), pltpu.VMEM((tm, 1), F32)],
        compiler_params=_cparams(("parallel", "arbitrary")),
        name="merge",
    )(ym, yh, yf, u_all, gate_up, gate_b, w_branch, w_out, x, g1, gate)


def _mlp_kernel(x_ref, g_ref, sh_ref, sc_ref, wu_ref, wd_ref, xb_ref, g3_ref, gt_ref, o_ref,
                hx_ref, h0_ref, h1_ref, acc_ref, inv_ref, *, nh, nb, tn):
    j = pl.program_id(1)

    def up(h_ref):
        h = jnp.dot(hx_ref[...], wu_ref[...], preferred_element_type=F32)
        h_ref[...] = jnp.square(jnp.maximum(h, 0.0)).astype(BF16)

    def down(h_ref):
        h = h_ref[...]
        for kk in range(nb):
            acc_ref[kk] = acc_ref[kk] + jnp.dot(h, wd_ref[:, kk * tn:(kk + 1) * tn], preferred_element_type=F32)

    @pl.when(j == 0)
    def _():
        hx_ref[...] = _rms_mod(x_ref[...], g_ref[...], sh_ref[...], sc_ref[...]).astype(BF16)
        acc_ref[...] = jnp.zeros(acc_ref.shape, F32)
        up(h0_ref)

    @pl.when((j > 0) & (j < nh) & (j % 2 == 1))
    def _():
        down(h0_ref)
        up(h1_ref)

    @pl.when((j > 0) & (j < nh) & (j % 2 == 0))
    def _():
        down(h1_ref)
        up(h0_ref)

    @pl.when(j == nh)
    def _():
        down(h1_ref if (nh - 1) % 2 == 1 else h0_ref)
        ss = jnp.sum(jnp.square(acc_ref[0]), axis=-1, keepdims=True)
        for kk in range(1, nb):
            ss = ss + jnp.sum(jnp.square(acc_ref[kk]), axis=-1, keepdims=True)
        inv_ref[...] = lax.rsqrt(ss * (1.0 / (nb * tn)) + EPS)

    @pl.when(j > nh)
    def _():
        o_ref[...] = xb_ref[...] + gt_ref[...] * ((acc_ref[j - nh - 1] * inv_ref[...]) * g3_ref[...])


def _mlp(x, g2, shift, scale, w_up, w_down, layer, g3, gate, tm, th, tn):
    m, d = x.shape
    dff = w_up.shape[2]
    nh = dff // th
    nb = d // tn
    row = lambda i, j: (0, 0)
    ph = lambda i, j: jnp.minimum(j, nh - 1)
    pd = lambda i, j: jnp.clip(j - 1, 0, nh - 1)
    pe = lambda i, j: jnp.clip(j - nh - 1, 0, nb - 1)
    return pl.pallas_call(
        functools.partial(_mlp_kernel, nh=nh, nb=nb, tn=tn),
        grid=(m // tm, nh + 1 + nb),
        in_specs=[pl.BlockSpec((tm, d), lambda i, j: (i, 0), pipeline_mode=pl.Buffered(1)),
                  pl.BlockSpec((1, d), row), pl.BlockSpec((1, d), row), pl.BlockSpec((1, d), row),
                  pl.BlockSpec((None, d, th), lambda i, j: (layer, 0, ph(i, j))),
                  pl.BlockSpec((None, th, d), lambda i, j: (layer, pd(i, j), 0)),
                  pl.BlockSpec((tm, tn), lambda i, j: (i, pe(i, j))),
                  pl.BlockSpec((1, tn), lambda i, j: (0, pe(i, j))),
                  pl.BlockSpec((1, tn), lambda i, j: (0, pe(i, j)))],
        out_specs=pl.BlockSpec((tm, tn), lambda i, j: (i, pe(i, j))),
        out_shape=jax.ShapeDtypeStruct((m, d), F32),
        scratch_shapes=[pltpu.VMEM((tm, d), BF16), pltpu.VMEM((tm, th), BF16), pltpu.VMEM((tm, th), BF16),
                        pltpu.VMEM((nb, tm, tn), F32), pltpu.VMEM((tm, 1), F32)],
        compiler_params=_cparams(("parallel", "arbitrary")),
        name="mlp",
    )(x, g2, shift, scale, w_up, w_down, x, g3, gate)


_DT_SHIFT = IN_SPLITS[2]


def _wprep_table():
    src, mode = [], []
    nblk = lambda w: w // LANES
    add = lambda a0, n, m: (src.extend(range(a0, a0 + n)), mode.extend([m] * n))
    add(0, nblk(BRANCH_W), 1)
    add((sum(IN_SPLITS[:7]) - _DT_SHIFT) // LANES, nblk(BRANCH_W), 0)
    add(BRANCH_W // LANES, nblk(XBC_W), 1)
    add((BRANCH_W + XBC_W) // LANES, nblk(HGRN_HK + 2 * HGRN_HK + 2 * BRANCH_W), 0)
    add((sum(IN_SPLITS[:8]) - _DT_SHIFT) // LANES, nblk(GATE_RANK), 0)
    add((BRANCH_W + XBC_W) // LANES, 1, 2)
    while len(src) < N_PAD // LANES:
        add(0, 1, 3)
    return np.asarray(src, np.int32), np.asarray(mode, np.int32)


def _wprep_kernel(src_ref, mode_ref, a_ref, b_ref, o_ref):
    mode = mode_ref[pl.program_id(0)]
    lanes = lax.broadcasted_iota(jnp.int32, a_ref.shape, 1)

    @pl.when(mode == 0)
    def _():
        ra = pltpu.roll(a_ref[...], LANES - _DT_SHIFT, 1)
        rb = pltpu.roll(b_ref[...], LANES - _DT_SHIFT, 1)
        o_ref[...] = jnp.where(lanes < LANES - _DT_SHIFT, ra, rb).astype(o_ref.dtype)

    @pl.when(mode == 1)
    def _():
        o_ref[...] = a_ref[...].astype(o_ref.dtype)

    @pl.when(mode == 2)
    def _():
        o_ref[...] = jnp.where(lanes < _DT_SHIFT, a_ref[...], 0.0).astype(o_ref.dtype)

    @pl.when(mode == 3)
    def _():
        o_ref[...] = jnp.zeros(o_ref.shape, o_ref.dtype)


def _permute_w_in(w, layer):
    _, d, n = w.shape
    src, mode = _wprep_table()
    last = (n - 1) // LANES
    return pl.pallas_call(
        _wprep_kernel,
        grid_spec=pltpu.PrefetchScalarGridSpec(
            num_scalar_prefetch=2,
            grid=(N_PAD // LANES,),
            in_specs=[pl.BlockSpec((None, d, LANES), lambda j, src, mode: (layer, 0, src[j])),
                      pl.BlockSpec((None, d, LANES), lambda j, src, mode: (layer, 0, jnp.minimum(src[j] + 1, last)))],
            out_specs=pl.BlockSpec((d, LANES), lambda j, src, mode: (0, j))),
        out_shape=jax.ShapeDtypeStruct((d, N_PAD), BF16),
        compiler_params=_cparams(("arbitrary",)),
        name="wprep",
    )(jnp.asarray(src), jnp.asarray(mode), w, w)


def _state_zeros():
    hm0 = jnp.zeros((SSM_GROUPS, SSM_HPG // 2, SSM_STATE, LANES), F32)
    sh0 = jnp.zeros((HGRN_PAIRS, 2 * HGRN_VAL, LANES), F32)
    return hm0, sh0


def kernel(x, c, ctx, c_ctx, ada_down, ada_up, ada_b, norm_g, w_in, conv_w, conv_b, ssm_dt_bias, ssm_a_log,
           ssm_d, ssm_norm, hgrn_lb_logits, hgrn_norm, gate_up, gate_b, w_branch, w_out, mlp_up, mlp_down):
    assert x.shape[0] == 1 and ctx.shape[0] == 1
    xs = x[0]
    cs = ctx[0]
    L, d = xs.shape
    lc = cs.shape[0]
    tm_x, tm_c = 512, lc
    p_lb = jax.nn.softmax(hgrn_lb_logits.astype(F32), axis=0)
    lower_bounds = jnp.clip(jnp.cumsum(p_lb, axis=0) - p_lb[0], 0.0, 1.0)
    cond = jnp.zeros((SUBLANES, d), F32).at[0].set(c[0]).at[1].set(c_ctx)
    cond = cond * jax.nn.sigmoid(cond)
    hm0, sh0 = _state_zeros()
    zero_b = jnp.zeros((1, ada_down.shape[-1]), F32)
    gu_all, gb_all = gate_up.astype(BF16), gate_b.astype(F32)[:, :, None, :]
    wbr_all, wo_all = w_branch.astype(BF16), w_out.astype(BF16)
    wu_all, wd_all = mlp_up.astype(BF16), mlp_down.astype(BF16)
    for l in range(DEPTH):
        last = l == DEPTH - 1
        g = norm_g[l].astype(F32)
        mods = _mm_small(_mm_small(cond, ada_down, zero_b, ada_down.shape[-1], l), ada_up, ada_b[l][None, :], 2048, l)
        sx1, cx1, gx1, sx2, cx2, gx2 = [mods[0:1, i * d:(i + 1) * d] for i in range(6)]
        sc1, cc1, gc1, sc2, cc2, gc2 = [mods[1:2, i * d:(i + 1) * d] for i in range(6)]
        w_in_l = _permute_w_in(w_in, l)
        mamba_p = (conv_w[l], conv_b[l], ssm_dt_bias[l], ssm_a_log[l], ssm_d[l], ssm_norm[l])

        uc = _inproj(cs, g[0:1], sc1, cc1, w_in_l, tm_c, 768)
        ym_c, hm_f, hm_b = _mamba(uc, *mamba_p, hm0, hm0)
        yh_c, sh_f, sh_b = _hgrn(uc, lower_bounds[l], hgrn_norm[l], sh0, sh0, 1)
        ux = _inproj(xs, g[0:1], sx1, cx1, w_in_l, tm_x, 768)
        ym, _, _ = _mamba(ux, *mamba_p, hm_f, hm_b)
        yh, _, _ = _hgrn(ux, lower_bounds[l], hgrn_norm[l], sh_f, sh_b, GRID_W)
        yf = _fnet(ux, GRID_W)
        xs = _merge(ym, yh, yf, ux, gu_all, gb_all, wbr_all, wo_all, l, xs, g[1:2], gx1, tm_x, 512)
        xs = _mlp(xs, g[2:3], sx2, cx2, wu_all, wd_all, l, g[3:4], gx2, tm_x, 512, 512)
        if not last:
            yf_c = _fnet_small(uc)
            cs = _merge(ym_c, yh_c, yf_c, uc, gu_all, gb_all, wbr_all, wo_all, l, cs, g[1:2], gc1, tm_c, 512)
            cs = _mlp(cs, g[2:3], sc2, cc2, wu_all, wd_all, l, g[3:4], gc2, tm_c, 512, 512)
    return xs[None]
```

```python
import functools
import math

import numpy as np
import jax
import jax.numpy as jnp
from jax import lax
from jax.experimental import pallas as pl
from jax.experimental.pallas import tpu as pltpu

F32 = jnp.float32
BF16 = jnp.bfloat16

D_MODEL = 4096
DEPTH = 4
GRID_W = 64
BRANCH_W = 1536
SSM_HEAD_DIM = 64
SSM_HEADS = 24
SSM_GROUPS = 4
SSM_HPG = 6
SSM_STATE = 128
CONV_W = 5
XBC_W = BRANCH_W + 2 * SSM_GROUPS * SSM_STATE
HGRN_VAL = 128
HGRN_HEADS = 12
HGRN_KEY = 64
HGRN_HK = HGRN_HEADS * HGRN_KEY
HGRN_PAIRS = HGRN_HEADS // 2
FOURIER_GROUPS = 4
FOURIER_GW = BRANCH_W // FOURIER_GROUPS
GATE_RANK = 256
D_FF = 4 * D_MODEL
CHUNK = 128
EPS = 1e-6
F_MIN = 1e-30
IN_SPLITS = (BRANCH_W, XBC_W, 2 * SSM_HEADS, HGRN_HK, 2 * HGRN_HK, BRANCH_W, BRANCH_W, BRANCH_W, GATE_RANK)

LANES = 128
SUBLANES = 8
VMEM_LIMIT = 56 * 1024 * 1024

OFF_Z = 0
OFF_FOUR = OFF_Z + BRANCH_W
OFF_XBC = OFF_FOUR + BRANCH_W
OFF_Q = OFF_XBC + XBC_W
OFF_F = OFF_Q + HGRN_HK
OFF_I = OFF_F + 2 * HGRN_HK
OFF_G = OFF_I + BRANCH_W
OFF_GATE = OFF_G + BRANCH_W
OFF_DT = OFF_GATE + GATE_RANK
N_PAD = 11520
NEG_BIG = -1e30
GLA_PP = 6


def _cparams(sem):
    return pltpu.CompilerParams(dimension_semantics=sem, vmem_limit_bytes=VMEM_LIMIT)


def _sigmoid(x):
    return 1.0 / (1.0 + jnp.exp(-x))


def _silu(x):
    return x * _sigmoid(x)


def _softplus(x):
    return jnp.maximum(x, 0.0) + jnp.log1p(jnp.exp(-jnp.abs(x)))


def _split3(x):
    hi = x.astype(BF16)
    r1 = x - hi.astype(F32)
    mid = r1.astype(BF16)
    lo = (r1 - mid.astype(F32)).astype(BF16)
    return hi, mid, lo


def _split2(x):
    hi = x.astype(BF16)
    return hi, (x - hi.astype(F32)).astype(BF16)


def _dot01_l(m01, x):
    hi, mid, lo = _split3(x)
    d = lambda a: jnp.dot(m01, a, preferred_element_type=F32)
    return d(hi) + d(mid) + d(lo)


def _dot_nt(a, b):
    return lax.dot_general(a, b, (((1,), (1,)), ((), ())), preferred_element_type=F32)


def _rms_mod(x, g, shift, scale):
    ms = jnp.mean(x * x, axis=-1, keepdims=True)
    y = (x * lax.rsqrt(ms + EPS)) * g
    return y * (1.0 + scale) + shift


def _mm_small_kernel(a_ref, w_ref, b_ref, o_ref):
    o_ref[...] = jnp.dot(a_ref[...].astype(BF16), w_ref[...].astype(BF16),
                         preferred_element_type=F32) + b_ref[...]


def _mm_small(a, w, b, tn, layer):
    m, k = a.shape
    n = w.shape[2]
    return pl.pallas_call(
        _mm_small_kernel,
        grid=(n // tn,),
        in_specs=[pl.BlockSpec((m, k), lambda j: (0, 0)),
                  pl.BlockSpec((None, k, tn), lambda j: (layer, 0, j)),
                  pl.BlockSpec((1, tn), lambda j: (0, j))],
        out_specs=pl.BlockSpec((m, tn), lambda j: (0, j)),
        out_shape=jax.ShapeDtypeStruct((m, n), F32),
        compiler_params=_cparams(("parallel",)),
        name="mm_small",
    )(a, w, b)


def _inproj_kernel(x_ref, g_ref, sh_ref, sc_ref, w_ref, o_ref, hx_ref):
    @pl.when(pl.program_id(1) == 0)
    def _():
        hx_ref[...] = _rms_mod(x_ref[...], g_ref[...], sh_ref[...], sc_ref[...]).astype(BF16)

    o_ref[...] = jnp.dot(hx_ref[...], w_ref[...], preferred_element_type=F32).astype(o_ref.dtype)


def _inproj(x, g, shift, scale, w, tm, tn):
    m, d = x.shape
    n = w.shape[1]
    row = lambda i, j: (0, 0)
    return pl.pallas_call(
        _inproj_kernel,
        grid=(m // tm, n // tn),
        in_specs=[pl.BlockSpec((tm, d), lambda i, j: (i, 0)),
                  pl.BlockSpec((1, d), row), pl.BlockSpec((1, d), row), pl.BlockSpec((1, d), row),
                  pl.BlockSpec((d, tn), lambda i, j: (0, j))],
        out_specs=pl.BlockSpec((tm, tn), lambda i, j: (i, j)),
        out_shape=jax.ShapeDtypeStruct((m, n), F32),
        scratch_shapes=[pltpu.VMEM((tm, d), BF16)],
        compiler_params=_cparams(("parallel", "arbitrary")),
        name="inproj",
    )(x, g, shift, scale, w)


def _ssd_kernel(*refs, reverse, final, direction):
    if final:
        (xs_ref, b_ref, c_ref, dt_ref, dtb_ref, a_ref, exp_ref, h0_ref,
         yp_ref, z_ref, dsk_ref, nrm_ref, y_ref, hfin_ref, st_ref) = refs
    else:
        (xs_ref, b_ref, c_ref, dt_ref, dtb_ref, a_ref, exp_ref, h0_ref, y_ref, hfin_ref, st_ref) = refs
    c = pl.program_id(0)
    nc = pl.num_programs(0)
    gw = BRANCH_W // SSM_GROUPS

    @pl.when(c == 0)
    def _():
        st_ref[...] = h0_ref[...]

    rows = lax.broadcasted_iota(jnp.int32, (CHUNK, CHUNK), 0)
    cols = lax.broadcasted_iota(jnp.int32, (CHUNK, CHUNK), 1)
    tri = (rows <= cols) if reverse else (rows >= cols)
    tri_bf = tri.astype(BF16)
    lane_lo = cols < SSM_HEAD_DIM
    end_row = 0 if reverse else CHUNK - 1

    dt_all = _softplus(dt_ref[...] + dtb_ref[...])
    dta = dt_all * a_ref[...]
    acum = _dot01_l(tri_bf, dta)
    acum_t = acum.T
    a_hi, a_lo = _split2(acum)
    dt_bf = dt_all.astype(BF16)

    for g in range(SSM_GROUPS):
        exp_m = exp_ref[g]
        acum_b = (jnp.dot(a_hi, exp_m, preferred_element_type=F32)
                  + jnp.dot(a_lo, exp_m, preferred_element_type=F32))
        dt_b = jnp.dot(dt_bf, exp_m, preferred_element_type=F32)
        xs = xs_ref[:, gw * g:gw * (g + 1)].astype(F32)
        bm = b_ref[:, SSM_STATE * g:SSM_STATE * (g + 1)]
        cm = c_ref[:, SSM_STATE * g:SSM_STATE * (g + 1)]
        bm_t = bm.astype(F32).T.astype(BF16)
        scores = _dot_nt(cm, bm)
        ys = []
        for j in range(SSM_HPG // 2):
            r0, r1 = 2 * j, 2 * j + 1
            col0 = direction * SSM_HEADS + g * SSM_HPG + r0
            ab0 = acum_b[:, LANES * r0:LANES * (r0 + 1)]
            ab1 = acum_b[:, LANES * r1:LANES * (r1 + 1)]
            acum_e = jnp.where(lane_lo, ab0, ab1)
            dt_e = jnp.where(lane_lo, dt_b[:, LANES * r0:LANES * (r0 + 1)], dt_b[:, LANES * r1:LANES * (r1 + 1)])
            xp = xs[:, LANES * j:LANES * (j + 1)] * dt_e
            end = acum_e[end_row:end_row + 1, :]
            xw = (xp * jnp.exp(end - acum_e)).astype(BF16)
            l0 = jnp.exp(jnp.where(tri, ab0 - acum_t[col0:col0 + 1, :], NEG_BIG))
            l1 = jnp.exp(jnp.where(tri, ab1 - acum_t[col0 + 1:col0 + 2, :], NEG_BIG))
            m0 = (scores * l0).astype(BF16)
            m1 = (scores * l1).astype(BF16)
            xp0 = jnp.where(lane_lo, xp, 0.0).astype(BF16)
            xp1 = jnp.where(lane_lo, 0.0, xp).astype(BF16)
            y_diag = (jnp.dot(m0, xp0, preferred_element_type=F32)
                      + jnp.dot(m1, xp1, preferred_element_type=F32))
            st = st_ref[g, j]
            y_off = jnp.dot(cm, st.astype(BF16), preferred_element_type=F32) * jnp.exp(acum_e)
            ys.append(y_diag + y_off)
            st_ref[g, j] = st * jnp.exp(end) + jnp.dot(bm_t, xw, preferred_element_type=F32)
        y = jnp.concatenate(ys, axis=1)
        if final:
            gs = slice(gw * g, gw * (g + 1))
            y = y + yp_ref[:, gs] + xs * dsk_ref[:, gs]
            y = y * _silu(z_ref[:, gs])
            y = y * lax.rsqrt(jnp.mean(y * y, axis=-1, keepdims=True) + EPS)
            y = y * nrm_ref[:, gs]
        y_ref[:, gw * g:gw * (g + 1)] = y.astype(y_ref.dtype)

    @pl.when(c == nc - 1)
    def _():
        hfin_ref[...] = st_ref[...]


def _ssd_tables(direction):
    exp_m = np.zeros((SSM_GROUPS, LANES, SSM_HPG * LANES), np.float32)
    for g in range(SSM_GROUPS):
        for r in range(SSM_HPG):
            col = direction * SSM_HEADS + g * SSM_HPG + r
            exp_m[g, col, LANES * r:LANES * (r + 1)] = 1.0
    return jnp.asarray(exp_m, BF16)


def _ssd_pass(xbc, u_all, dtb_row, a_row, h0, direction, final_args=None):
    L = xbc.shape[0]
    nc = L // CHUNK
    reverse = direction == 1
    final = final_args is not None
    exp_m = _ssd_tables(direction)
    bc_w = SSM_GROUPS * SSM_STATE
    cidx = (lambda c: nc - 1 - c) if reverse else (lambda c: c)
    st_shape = (SSM_GROUPS, SSM_HPG // 2, SSM_STATE, LANES)
    in_specs = [
        pl.BlockSpec((CHUNK, BRANCH_W), lambda c: (cidx(c), 0)),
        pl.BlockSpec((CHUNK, bc_w), lambda c: (cidx(c), BRANCH_W // bc_w)),
        pl.BlockSpec((CHUNK, bc_w), lambda c: (cidx(c), BRANCH_W // bc_w + 1)),
        pl.BlockSpec((CHUNK, LANES), lambda c: (cidx(c), OFF_DT // LANES)),
        pl.BlockSpec((1, LANES), lambda c: (0, 0)),
        pl.BlockSpec((1, LANES), lambda c: (0, 0)),
        pl.BlockSpec((SSM_GROUPS, LANES, SSM_HPG * LANES), lambda c: (0, 0, 0)),
        pl.BlockSpec(st_shape, lambda c: (0, 0, 0, 0)),
    ]
    args = [xbc, xbc, xbc, u_all, dtb_row, a_row, exp_m, h0]
    if final:
        y_prev, dsk_row, nrm_row = final_args
        in_specs += [
            pl.BlockSpec((CHUNK, BRANCH_W), lambda c: (cidx(c), 0)),
            pl.BlockSpec((CHUNK, BRANCH_W), lambda c: (cidx(c), OFF_Z // BRANCH_W)),
            pl.BlockSpec((1, BRANCH_W), lambda c: (0, 0)),
            pl.BlockSpec((1, BRANCH_W), lambda c: (0, 0)),
        ]
        args += [y_prev, u_all, dsk_row, nrm_row]
    out_dtype = BF16 if final else F32
    y, hfin = pl.pallas_call(
        functools.partial(_ssd_kernel, reverse=reverse, final=final, direction=direction),
        grid=(nc,),
        in_specs=in_specs,
        out_specs=[pl.BlockSpec((CHUNK, BRANCH_W), lambda c: (cidx(c), 0)),
                   pl.BlockSpec(st_shape, lambda c: (0, 0, 0, 0))],
        out_shape=[jax.ShapeDtypeStruct((L, BRANCH_W), out_dtype),
                   jax.ShapeDtypeStruct(st_shape, F32)],
        scratch_shapes=[pltpu.VMEM(st_shape, F32)],
        compiler_params=_cparams(("arbitrary",)),
        name="ssd_bwd" if reverse else "ssd_fwd",
    )(*args)
    return y, hfin


def _conv_kernel(u_ref, up_ref, un_ref, w_ref, b_ref, o_ref, *, tm):
    i = pl.program_id(0)
    halo = SUBLANES
    n = tm + 2 * halo
    cur = u_ref[...]
    prev = jnp.where(i > 0, up_ref[...], 0.0)
    nxt = jnp.where(i < pl.num_programs(0) - 1, un_ref[...], 0.0)
    ext = jnp.concatenate([prev, cur, nxt], axis=0)
    mid = CONV_W // 2
    acc = b_ref[...] + w_ref[mid:mid + 1, :] * cur
    for k in range(CONV_W):
        d = k - mid
        if d != 0:
            acc = acc + w_ref[k:k + 1, :] * pltpu.roll(ext, (-d) % n, 0)[halo:halo + tm]
    o_ref[...] = (acc * _sigmoid(acc)).astype(o_ref.dtype)


def _conv_silu(u_all, conv_w, conv_b, tm):
    L = u_all.shape[0]
    cw = SSM_GROUPS * SSM_STATE
    hb = tm // SUBLANES
    last = L // SUBLANES - 1
    c0 = OFF_XBC // cw
    return pl.pallas_call(
        functools.partial(_conv_kernel, tm=tm),
        grid=(L // tm, XBC_W // cw),
        in_specs=[pl.BlockSpec((tm, cw), lambda i, j: (i, c0 + j)),
                  pl.BlockSpec((SUBLANES, cw), lambda i, j: (jnp.maximum(i * hb - 1, 0), c0 + j)),
                  pl.BlockSpec((SUBLANES, cw), lambda i, j: (jnp.minimum((i + 1) * hb, last), c0 + j)),
                  pl.BlockSpec((CONV_W, cw), lambda i, j: (0, j)),
                  pl.BlockSpec((1, cw), lambda i, j: (0, j))],
        out_specs=pl.BlockSpec((tm, cw), lambda i, j: (i, j)),
        out_shape=jax.ShapeDtypeStruct((L, XBC_W), BF16),
        compiler_params=_cparams(("parallel", "parallel")),
        name="conv_silu",
    )(u_all, u_all, u_all, conv_w.astype(F32), conv_b.astype(F32)[None, :])


def _pad_row(v, width=LANES):
    return jnp.zeros((1, width), F32).at[0, :v.shape[0]].set(v.astype(F32))


def _mamba(u_all, conv_w, conv_b, dt_bias, a_log, d_skip, ssm_norm, h0_f, h0_b):
    xbc = _conv_silu(u_all, conv_w, conv_b, min(u_all.shape[0], 512))
    dtb_row = _pad_row(dt_bias.reshape(-1))
    a_row = _pad_row(-jnp.exp(a_log.astype(F32)).reshape(-1))
    dsk_row = jnp.repeat(d_skip.astype(F32), SSM_HEAD_DIM)[None, :]
    nrm_row = ssm_norm.astype(F32)[None, :]
    y_f, hf = _ssd_pass(xbc, u_all, dtb_row, a_row, h0_f, 0)
    y, hb = _ssd_pass(xbc, u_all, dtb_row, a_row, h0_b, 1, (y_f, dsk_row, nrm_row))
    return y, hf, hb


def _gla_chunk(uq, uf, v, lb, st_t, ebc, sel_small, bd_mask, reverse):
    Q = CHUNK
    rows = lax.broadcasted_iota(jnp.int32, (Q, LANES), 0)
    lanes = lax.broadcasted_iota(jnp.int32, (Q, LANES), 1)
    r2 = lax.broadcasted_iota(jnp.int32, (Q, Q), 0)
    c2 = lax.broadcasted_iota(jnp.int32, (Q, Q), 1)
    tri_bf = ((r2 <= c2) if reverse else (r2 >= c2)).astype(BF16)
    lane_lo = lanes < HGRN_KEY
    end_row = 0 if reverse else Q - 1

    q = _silu(uq)
    f = lb + (1.0 - lb) * _sigmoid(uf)
    logf = jnp.log(jnp.maximum(f, F_MIN))
    k = (1.0 - lb) * _sigmoid(-uf)
    gc = _dot01_l(tri_bf, logf)

    xr = r2 ^ c2
    att = jnp.dot((q * k).astype(BF16), ebc, preferred_element_type=F32)
    a0 = jnp.where(xr == 0, att[:, :HGRN_VAL], 0.0)
    a1 = jnp.where(xr == 0, att[:, HGRN_VAL:], 0.0)
    b = 1
    lvl = 0
    while b < Q:
        if 2 * b < SUBLANES:
            ref = _dot01_l(sel_small[lvl], gc)
        else:
            pieces = []
            for p0 in range(0, Q, 2 * b):
                row = p0 + b if reverse else p0 + b - 1
                pieces.append(jnp.broadcast_to(gc[row:row + 1, :], (2 * b, LANES)))
            ref = jnp.concatenate(pieces, axis=0) if len(pieces) > 1 else pieces[0]
        hi = (rows & b) != 0
        is_q = (~hi) if reverse else hi
        qt = (q * jnp.exp(jnp.where(is_q, gc - ref, NEG_BIG))).astype(BF16)
        ek = ref - gc
        kt0 = (k * jnp.exp(jnp.where((~is_q) & lane_lo, ek, NEG_BIG))).astype(BF16)
        kt1 = (k * jnp.exp(jnp.where((~is_q) & (~lane_lo), ek, NEG_BIG))).astype(BF16)
        att = _dot_nt(qt, jnp.concatenate([kt0, kt1], axis=0))
        if 2 * b < Q:
            same = xr < 2 * b
            a0 = jnp.where(same & (xr >= b), att[:, :Q], a0)
            a1 = jnp.where(same & (xr >= b), att[:, Q:], a1)
        else:
            a0 = jnp.where(xr >= b, att[:, :Q], a0)
            a1 = jnp.where(xr >= b, att[:, Q:], a1)
        b *= 2
        lvl += 1
    vb = v.astype(BF16)
    o = jnp.concatenate([jnp.dot(a0.astype(BF16), vb[:, :HGRN_VAL], preferred_element_type=F32),
                         jnp.dot(a1.astype(BF16), vb[:, HGRN_VAL:], preferred_element_type=F32)], axis=1)

    qg = (q * jnp.exp(gc)).astype(BF16)
    o = o + _dot_nt(qg, st_t.astype(BF16))
    end = gc[end_row:end_row + 1, :]
    kd = (k * jnp.exp(end - gc)).astype(BF16)
    st_new = st_t * jnp.exp(end) + jnp.dot(v.T.astype(BF16), kd, preferred_element_type=F32) * bd_mask
    return o, st_new


def _gla_kernel(u_hbm, lb_ref, ebc_ref, sel_ref, s0_ref, o_hbm, sfin_ref, st_ref, bq, bf, bv, bo, isem, osem,
                *, reverse, direction, ncol, nrow_chunks):
    p = pl.program_id(0)
    c = pl.program_id(1)
    col = (ncol - 1 - c) if reverse else c
    nxt = (col - 1) if reverse else (col + 1)
    slot = c % 2
    vw = 2 * HGRN_VAL
    kw2, vw2 = GLA_PP * LANES, GLA_PP * vw
    q_off = pl.multiple_of(OFF_Q + p * kw2, LANES)
    f_off = pl.multiple_of(OFF_F + direction * HGRN_HK + p * kw2, LANES)
    v_off = pl.multiple_of(OFF_I + p * vw2, LANES)
    o_off = pl.multiple_of(p * vw2, LANES)

    def in_copies(col_, slot_):
        return (pltpu.make_async_copy(u_hbm.at[:, col_, pl.ds(q_off, kw2)], bq.at[slot_], isem.at[0, slot_]),
                pltpu.make_async_copy(u_hbm.at[:, col_, pl.ds(f_off, kw2)], bf.at[slot_], isem.at[1, slot_]),
                pltpu.make_async_copy(u_hbm.at[:, col_, pl.ds(v_off, vw2)], bv.at[slot_], isem.at[2, slot_]))

    def out_copy(col_, slot_):
        return pltpu.make_async_copy(bo.at[slot_], o_hbm.at[:, col_, pl.ds(o_off, vw2)], osem.at[slot_])

    @pl.when(c == 0)
    def _():
        st_ref[...] = s0_ref[...]
        for i, cp in enumerate(in_copies(col, slot)):
            cp.start(priority=i // 2)

    for cp in in_copies(col, slot):
        cp.wait()

    @pl.when(c + 1 < ncol)
    def _():
        for i, cp in enumerate(in_copies(nxt, 1 - slot)):
            cp.start(priority=i // 2)

    @pl.when(c >= 2)
    def _():
        out_copy(col, slot).wait()

    r2 = lax.broadcasted_iota(jnp.int32, (vw, LANES), 0)
    c2 = lax.broadcasted_iota(jnp.int32, (vw, LANES), 1)
    bd_mask = ((r2 < HGRN_VAL) == (c2 < HGRN_KEY)).astype(F32)
    ebc = ebc_ref[...]
    sel_small = [sel_ref[i] for i in range(sel_ref.shape[0])]
    order = range(nrow_chunks - 1, -1, -1) if reverse else range(nrow_chunks)
    for rc in order:
        sl = pl.ds(rc * CHUNK, CHUNK)
        for pp in range(GLA_PP):
            ks = slice(LANES * pp, LANES * (pp + 1))
            vs = slice(vw * pp, vw * (pp + 1))
            o, st_new = _gla_chunk(bq[slot, sl, ks], bf[slot, sl, ks], bv[slot, sl, vs], lb_ref[:, ks], st_ref[pp],
                                   ebc, sel_small, bd_mask, reverse)
            st_ref[pp] = st_new
            bo[slot, sl, vs] = o

    out_copy(col, slot).start()

    @pl.when(c == ncol - 1)
    def _():
        out_copy(col, slot).wait()
        if ncol >= 2:
            out_copy(col, 1 - slot).wait()
        sfin_ref[...] = st_ref[...]


def _gla_pass(u3, lb_rows, s0, direction):
    R, Wc, _ = u3.shape
    reverse = direction == 1
    ebc = np.zeros((LANES, 2 * HGRN_VAL), np.float32)
    ebc[:HGRN_KEY, :HGRN_VAL] = 1.0
    ebc[HGRN_KEY:, HGRN_VAL:] = 1.0
    ebc = jnp.asarray(ebc, BF16)
    small = [b for b in (1, 2, 4) if 2 * b < SUBLANES]
    sel = np.zeros((len(small), CHUNK, CHUNK), np.float32)
    for i, b in enumerate(small):
        for r in range(CHUNK):
            p0 = (r // (2 * b)) * 2 * b
            sel[i, r, p0 + b if reverse else p0 + b - 1] = 1.0
    sel = jnp.asarray(sel, BF16)
    vw = 2 * HGRN_VAL
    o, sfin = pl.pallas_call(
        functools.partial(_gla_kernel, reverse=reverse, direction=direction, ncol=Wc, nrow_chunks=R // CHUNK),
        grid=(HGRN_PAIRS // GLA_PP, Wc),
        in_specs=[pl.BlockSpec(memory_space=pl.ANY),
                  pl.BlockSpec((None, 1, GLA_PP * LANES), lambda p, c: (p, 0, 0)),
                  pl.BlockSpec((LANES, vw), lambda p, c: (0, 0)),
                  pl.BlockSpec((len(small), CHUNK, CHUNK), lambda p, c: (0, 0, 0)),
                  pl.BlockSpec((GLA_PP, vw, LANES), lambda p, c: (p, 0, 0))],
        out_specs=[pl.BlockSpec(memory_space=pl.ANY),
                   pl.BlockSpec((GLA_PP, vw, LANES), lambda p, c: (p, 0, 0))],
        out_shape=[jax.ShapeDtypeStruct((R, Wc, BRANCH_W), F32),
                   jax.ShapeDtypeStruct((HGRN_PAIRS, vw, LANES), F32)],
        scratch_shapes=[pltpu.VMEM((GLA_PP, vw, LANES), F32),
                        pltpu.VMEM((2, R, GLA_PP * LANES), F32), pltpu.VMEM((2, R, GLA_PP * LANES), F32),
                        pltpu.VMEM((2, R, GLA_PP * vw), F32), pltpu.VMEM((2, R, GLA_PP * vw), F32),
                        pltpu.SemaphoreType.DMA((3, 2)), pltpu.SemaphoreType.DMA((2,))],
        compiler_params=_cparams(("arbitrary", "arbitrary")),
        name="gla_bwd" if reverse else "gla_fwd",
    )(u3, lb_rows, ebc, sel, s0)
    return o, sfin


def _hgrn_post_kernel(of_ref, ob_ref, g_ref, nrm_ref, y_ref):
    o = of_ref[...] + ob_ref[...]
    gate = _silu(g_ref[...])
    nrm = nrm_ref[...]
    for h in range(2):
        sl = slice(HGRN_VAL * h, HGRN_VAL * (h + 1))
        oh = o[:, sl]
        oh = oh * lax.rsqrt(jnp.mean(oh * oh, axis=-1, keepdims=True) + EPS)
        y_ref[:, sl] = (oh * nrm * gate[:, sl]).astype(y_ref.dtype)


def _hgrn_post(o_f, o_b, u_all, nrm_row, tm):
    L = o_f.shape[0]
    vw = 2 * HGRN_VAL
    blk = pl.BlockSpec((tm, vw), lambda i, p: (i, p))
    return pl.pallas_call(
        _hgrn_post_kernel,
        grid=(L // tm, HGRN_PAIRS),
        in_specs=[blk, blk, pl.BlockSpec((tm, vw), lambda i, p: (i, OFF_G // vw + p)),
                  pl.BlockSpec((1, HGRN_VAL), lambda i, p: (0, 0))],
        out_specs=blk,
        out_shape=jax.ShapeDtypeStruct((L, BRANCH_W), BF16),
        compiler_params=_cparams(("parallel", "parallel")),
        name="hgrn_post",
    )(o_f, o_b, u_all, nrm_row)


def _hgrn(u_all, lower_bound, hgrn_norm, s0_f, s0_b, grid_w):
    L = u_all.shape[0]
    R = L // grid_w
    u3 = u_all.reshape(R, grid_w, N_PAD)
    lb = lower_bound.astype(F32).reshape(2, HGRN_PAIRS // GLA_PP, 1, GLA_PP * LANES)
    o_f, sf = _gla_pass(u3, lb[0], s0_f, 0)
    o_b, sb = _gla_pass(u3, lb[1], s0_b, 1)
    y = _hgrn_post(o_f.reshape(L, BRANCH_W), o_b.reshape(L, BRANCH_W), u_all, hgrn_norm.astype(F32)[None, :],
                   min(L, 1024))
    return y, sf, sb


def _dft_tables(R, Wc):
    L = R * Wc
    q = np.arange(R)[None, :, None]
    r = np.arange(R)[None, None, :]
    c = np.arange(Wc)[:, None, None]
    ang = 2.0 * np.pi * ((q * (Wc * r + c)) % L) / L
    m = np.concatenate([np.cos(ang), -np.sin(ang)], axis=1)
    qc = np.arange(Wc)[:, None]
    cc = np.arange(Wc)[None, :]
    a2 = 2.0 * np.pi * ((qc * cc) % Wc) / Wc
    t = np.block([[np.cos(a2), np.sin(a2)], [np.sin(a2), -np.cos(a2)]])
    ch = np.arange(FOURIER_GW)
    a3 = 2.0 * np.pi * ((ch[:, None] * ch[None, :]) % FOURIER_GW) / FOURIER_GW
    cs = np.concatenate([np.cos(a3), -np.sin(a3)], axis=0) / math.sqrt(L * FOURIER_GW)
    return (jnp.asarray(m, F32).astype(BF16), jnp.asarray(t, F32).astype(BF16), jnp.asarray(cs, F32).astype(BF16))


def _fnet_a_kernel(x_ref, m_ref, y_ref, *, wb, R):
    for jj in range(wb):
        res = jnp.dot(m_ref[jj], x_ref[:, jj, :].astype(BF16), preferred_element_type=F32)
        y_ref[:, 0, jj, :] = res[:R]
        y_ref[:, 1, jj, :] = res[R:]


def _fnet_b_kernel(y_ref, t_ref, cs_ref, o_ref, *, Wc, rb):
    t = t_ref[...]
    cs = cs_ref[...]
    for i in range(rb):
        yy = y_ref[i].reshape(2 * Wc, FOURIER_GW).astype(BF16)
        pq = jnp.dot(t, yy, preferred_element_type=F32)
        lhs = jnp.concatenate([pq[:Wc], pq[Wc:]], axis=1).astype(BF16)
        o_ref[:, i, :] = jnp.dot(lhs, cs, preferred_element_type=F32).astype(o_ref.dtype)


def _fnet(u_all, grid_w):
    L = u_all.shape[0]
    R = L // grid_w
    Wc = grid_w
    wb = SUBLANES
    rb = SUBLANES
    m, t, cs = _dft_tables(R, Wc)
    u3 = u_all.reshape(R, Wc, N_PAD)
    y = pl.pallas_call(
        functools.partial(_fnet_a_kernel, wb=wb, R=R),
        grid=(Wc // wb, FOURIER_GROUPS),
        in_specs=[pl.BlockSpec((R, wb, FOURIER_GW), lambda cb, g: (0, cb, OFF_FOUR // FOURIER_GW + g)),
                  pl.BlockSpec((wb, 2 * R, R), lambda cb, g: (cb, 0, 0))],
        out_specs=pl.BlockSpec((R, 2, wb, FOURIER_GW), lambda cb, g: (0, 0, cb, g)),
        out_shape=jax.ShapeDtypeStruct((R, 2, Wc, BRANCH_W), F32),
        compiler_params=_cparams(("parallel", "arbitrary")),
        name="fnet_a",
    )(u3, m)
    out = pl.pallas_call(
        functools.partial(_fnet_b_kernel, Wc=Wc, rb=rb),
        grid=(R // rb, FOURIER_GROUPS),
        in_specs=[pl.BlockSpec((rb, 2, Wc, FOURIER_GW), lambda qb, g: (qb, 0, 0, g)),
                  pl.BlockSpec((2 * Wc, 2 * Wc), lambda qb, g: (0, 0)),
                  pl.BlockSpec((2 * FOURIER_GW, FOURIER_GW), lambda qb, g: (0, 0))],
        out_specs=pl.BlockSpec((Wc, rb, FOURIER_GW), lambda qb, g: (0, qb, g)),
        out_shape=jax.ShapeDtypeStruct((Wc, R, BRANCH_W), BF16),
        compiler_params=_cparams(("parallel", "arbitrary")),
        name="fnet_b",
    )(y, t, cs)
    return out.reshape(L, BRANCH_W)


def _fnet_small_kernel(x_ref, m_ref, cs_ref, o_ref, *, L):
    pq = jnp.dot(m_ref[...], x_ref[...].astype(BF16), preferred_element_type=F32)
    lhs = jnp.concatenate([pq[:L], -pq[L:]], axis=1).astype(BF16)
    o_ref[...] = jnp.dot(lhs, cs_ref[...], preferred_element_type=F32).astype(o_ref.dtype)


def _fnet_small(u_all):
    L = u_all.shape[0]
    m, _, cs = _dft_tables(L, 1)
    return pl.pallas_call(
        functools.partial(_fnet_small_kernel, L=L),
        grid=(FOURIER_GROUPS,),
        in_specs=[pl.BlockSpec((L, FOURIER_GW), lambda g: (0, OFF_FOUR // FOURIER_GW + g)),
                  pl.BlockSpec((2 * L, L), lambda g: (0, 0)),
                  pl.BlockSpec((2 * FOURIER_GW, FOURIER_GW), lambda g: (0, 0))],
        out_specs=pl.BlockSpec((L, FOURIER_GW), lambda g: (0, g)),
        out_shape=jax.ShapeDtypeStruct((L, BRANCH_W), BF16),
        compiler_params=_cparams(("parallel",)),
        name="fnet_small",
    )(u_all, m[0], cs)


def _merge_kernel(ym_ref, yh_ref, yf_ref, ug_ref, gu_ref, gb_ref, wbr_ref, wo_ref, x_ref, g1_ref, gt_ref,
                  o_ref, mg_ref, acc_ref, ss_ref, *, nb, tn):
    j = pl.program_id(1)

    @pl.when(j < nb)
    def _():
        ug = ug_ref[...].astype(BF16)
        tot = None
        for i, y_ref in enumerate((ym_ref, yh_ref, yf_ref)):
            gate = _sigmoid(jnp.dot(ug, gu_ref[i], preferred_element_type=F32) + gb_ref[i])
            term = gate * jnp.dot(y_ref[...], wbr_ref[i], preferred_element_type=F32)
            tot = term if tot is None else tot + term
        mg_ref[j] = tot.astype(BF16)

    @pl.when((j >= nb) & (j < 2 * nb))
    def _():
        acc = jnp.dot(mg_ref[0], wo_ref[0:tn, :], preferred_element_type=F32)
        for kk in range(1, nb):
            acc = acc + jnp.dot(mg_ref[kk], wo_ref[kk * tn:(kk + 1) * tn, :], preferred_element_type=F32)
        acc_ref[j - nb] = acc
        part = jnp.sum(acc * acc, axis=-1, keepdims=True)

        @pl.when(j == nb)
        def _():
            ss_ref[...] = part

        @pl.when(j > nb)
        def _():
            ss_ref[...] = ss_ref[...] + part

    @pl.when(j >= 2 * nb)
    def _():
        inv = lax.rsqrt(ss_ref[...] * (1.0 / (nb * tn)) + EPS)
        o_ref[...] = x_ref[...] + gt_ref[...] * ((acc_ref[j - 2 * nb] * inv) * g1_ref[...])


def _merge(ym, yh, yf, u_all, gate_up, gate_b, w_branch, w_out, layer, x, g1, gate, tm, tn):
    m, d = x.shape
    nb = d // tn
    p1 = lambda i, j: jnp.minimum(j, nb - 1)
    p2 = lambda i, j: jnp.clip(j - nb, 0, nb - 1)
    p3 = lambda i, j: jnp.clip(j - 2 * nb, 0, nb - 1)
    ybs = pl.BlockSpec((tm, BRANCH_W), lambda i, j: (i, 0))
    return pl.pallas_call(
        functools.partial(_merge_kernel, nb=nb, tn=tn),
        grid=(m // tm, 3 * nb),
        in_specs=[ybs, ybs, ybs,
                  pl.BlockSpec((tm, GATE_RANK), lambda i, j: (i, OFF_GATE // GATE_RANK)),
                  pl.BlockSpec((None, 3, GATE_RANK, tn), lambda i, j: (layer, 0, 0, p1(i, j))),
                  pl.BlockSpec((None, 3, 1, tn), lambda i, j: (layer, 0, 0, p1(i, j))),
                  pl.BlockSpec((None, 3, BRANCH_W, tn), lambda i, j: (layer, 0, 0, p1(i, j))),
                  pl.BlockSpec((None, d, tn), lambda i, j: (layer, 0, p2(i, j))),
                  pl.BlockSpec((tm, tn), lambda i, j: (i, p3(i, j))),
                  pl.BlockSpec((1, tn), lambda i, j: (0, p3(i, j))),
                  pl.BlockSpec((1, tn), lambda i, j: (0, p3(i, j)))],
        out_specs=pl.BlockSpec((tm, tn), lambda i, j: (i, p3(i, j))),
        out_shape=jax.ShapeDtypeStruct((m, d), F32),
        scratch_shapes=[pltpu.VMEM((nb, tm, tn), BF16), pltpu.VMEM((nb, tm, tn), F32), pltpu.VMEM((tm, 1), F32)],
        compiler_params=_cparams(("parallel", "arbitrary")),
        name="merge",
    )(ym, yh, yf, u_all, gate_up, gate_b, w_branch, w_out, x, g1, gate)


def _mlp_kernel(x_ref, g_ref, sh_ref, sc_ref, wu_ref, wd_ref, xb_ref, g3_ref, gt_ref, o_ref,
                hx_ref, h0_ref, h1_ref, acc_ref, inv_ref, *, nh, nb, tn):
    j = pl.program_id(1)

    def up(h_ref):
        h = jnp.dot(hx_ref[...], wu_ref[...], preferred_element_type=F32)
        h_ref[...] = jnp.square(jnp.maximum(h, 0.0)).astype(BF16)

    def down(h_ref):
        h = h_ref[...]
        for kk in range(nb):
            acc_ref[kk] = acc_ref[kk] + jnp.dot(h, wd_ref[:, kk * tn:(kk + 1) * tn], preferred_element_type=F32)

    @pl.when(j == 0)
    def _():
        hx_ref[...] = _rms_mod(x_ref[...], g_ref[...], sh_ref[...], sc_ref[...]).astype(BF16)
        acc_ref[...] = jnp.zeros(acc_ref.shape, F32)
        up(h0_ref)

    @pl.when((j > 0) & (j < nh) & (j % 2 == 1))
    def _():
        down(h0_ref)
        up(h1_ref)

    @pl.when((j > 0) & (j < nh) & (j % 2 == 0))
    def _():
        down(h1_ref)
        up(h0_ref)

    @pl.when(j == nh)
    def _():
        down(h1_ref if (nh - 1) % 2 == 1 else h0_ref)
        ss = jnp.sum(jnp.square(acc_ref[0]), axis=-1, keepdims=True)
        for kk in range(1, nb):
            ss = ss + jnp.sum(jnp.square(acc_ref[kk]), axis=-1, keepdims=True)
        inv_ref[...] = lax.rsqrt(ss * (1.0 / (nb * tn)) + EPS)

    @pl.when(j > nh)
    def _():
        o_ref[...] = xb_ref[...] + gt_ref[...] * ((acc_ref[j - nh - 1] * inv_ref[...]) * g3_ref[...])


def _mlp(x, g2, shift, scale, w_up, w_down, layer, g3, gate, tm, th, tn):
    m, d = x.shape
    dff = w_up.shape[2]
    nh = dff // th
    nb = d // tn
    row = lambda i, j: (0, 0)
    ph = lambda i, j: jnp.minimum(j, nh - 1)
    pd = lambda i, j: jnp.clip(j - 1, 0, nh - 1)
    pe = lambda i, j: jnp.clip(j - nh - 1, 0, nb - 1)
    return pl.pallas_call(
        functools.partial(_mlp_kernel, nh=nh, nb=nb, tn=tn),
        grid=(m // tm, nh + 1 + nb),
        in_specs=[pl.BlockSpec((tm, d), lambda i, j: (i, 0), pipeline_mode=pl.Buffered(1)),
                  pl.BlockSpec((1, d), row), pl.BlockSpec((1, d), row), pl.BlockSpec((1, d), row),
                  pl.BlockSpec((None, d, th), lambda i, j: (layer, 0, ph(i, j))),
                  pl.BlockSpec((None, th, d), lambda i, j: (layer, pd(i, j), 0)),
                  pl.BlockSpec((tm, tn), lambda i, j: (i, pe(i, j))),
                  pl.BlockSpec((1, tn), lambda i, j: (0, pe(i, j))),
                  pl.BlockSpec((1, tn), lambda i, j: (0, pe(i, j)))],
        out_specs=pl.BlockSpec((tm, tn), lambda i, j: (i, pe(i, j))),
        out_shape=jax.ShapeDtypeStruct((m, d), F32),
        scratch_shapes=[pltpu.VMEM((tm, d), BF16), pltpu.VMEM((tm, th), BF16), pltpu.VMEM((tm, th), BF16),
                        pltpu.VMEM((nb, tm, tn), F32), pltpu.VMEM((tm, 1), F32)],
        compiler_params=_cparams(("parallel", "arbitrary")),
        name="mlp",
    )(x, g2, shift, scale, w_up, w_down, x, g3, gate)


_DT_SHIFT = IN_SPLITS[2]


def _wprep_table():
    src, mode = [], []
    nblk = lambda w: w // LANES
    add = lambda a0, n, m: (src.extend(range(a0, a0 + n)), mode.extend([m] * n))
    add(0, nblk(BRANCH_W), 1)
    add((sum(IN_SPLITS[:7]) - _DT_SHIFT) // LANES, nblk(BRANCH_W), 0)
    add(BRANCH_W // LANES, nblk(XBC_W), 1)
    add((BRANCH_W + XBC_W) // LANES, nblk(HGRN_HK + 2 * HGRN_HK + 2 * BRANCH_W), 0)
    add((sum(IN_SPLITS[:8]) - _DT_SHIFT) // LANES, nblk(GATE_RANK), 0)
    add((BRANCH_W + XBC_W) // LANES, 1, 2)
    while len(src) < N_PAD // LANES:
        add(0, 1, 3)
    return np.asarray(src, np.int32), np.asarray(mode, np.int32)


def _wprep_kernel(src_ref, mode_ref, a_ref, b_ref, o_ref):
    mode = mode_ref[pl.program_id(0)]
    lanes = lax.broadcasted_iota(jnp.int32, a_ref.shape, 1)

    @pl.when(mode == 0)
    def _():
        ra = pltpu.roll(a_ref[...], LANES - _DT_SHIFT, 1)
        rb = pltpu.roll(b_ref[...], LANES - _DT_SHIFT, 1)
        o_ref[...] = jnp.where(lanes < LANES - _DT_SHIFT, ra, rb).astype(o_ref.dtype)

    @pl.when(mode == 1)
    def _():
        o_ref[...] = a_ref[...].astype(o_ref.dtype)

    @pl.when(mode == 2)
    def _():
        o_ref[...] = jnp.where(lanes < _DT_SHIFT, a_ref[...], 0.0).astype(o_ref.dtype)

    @pl.when(mode == 3)
    def _():
        o_ref[...] = jnp.zeros(o_ref.shape, o_ref.dtype)


def _permute_w_in(w, layer):
    _, d, n = w.shape
    src, mode = _wprep_table()
    last = (n - 1) // LANES
    return pl.pallas_call(
        _wprep_kernel,
        grid_spec=pltpu.PrefetchScalarGridSpec(
            num_scalar_prefetch=2,
            grid=(N_PAD // LANES,),
            in_specs=[pl.BlockSpec((None, d, LANES), lambda j, src, mode: (layer, 0, src[j])),
                      pl.BlockSpec((None, d, LANES), lambda j, src, mode: (layer, 0, jnp.minimum(src[j] + 1, last)))],
            out_specs=pl.BlockSpec((d, LANES), lambda j, src, mode: (0, j))),
        out_shape=jax.ShapeDtypeStruct((d, N_PAD), BF16),
        compiler_params=_cparams(("arbitrary",)),
        name="wprep",
    )(jnp.asarray(src), jnp.asarray(mode), w, w)


def _state_zeros():
    hm0 = jnp.zeros((SSM_GROUPS, SSM_HPG // 2, SSM_STATE, LANES), F32)
    sh0 = jnp.zeros((HGRN_PAIRS, 2 * HGRN_VAL, LANES), F32)
    return hm0, sh0


def kernel(x, c, ctx, c_ctx, ada_down, ada_up, ada_b, norm_g, w_in, conv_w, conv_b, ssm_dt_bias, ssm_a_log,
           ssm_d, ssm_norm, hgrn_lb_logits, hgrn_norm, gate_up, gate_b, w_branch, w_out, mlp_up, mlp_down):
    assert x.shape[0] == 1 and ctx.shape[0] == 1
    xs = x[0]
    cs = ctx[0]
    L, d = xs.shape
    lc = cs.shape[0]
    tm_x, tm_c = 512, lc
    p_lb = jax.nn.softmax(hgrn_lb_logits.astype(F32), axis=0)
    lower_bounds = jnp.clip(jnp.cumsum(p_lb, axis=0) - p_lb[0], 0.0, 1.0)
    cond = jnp.zeros((SUBLANES, d), F32).at[0].set(c[0]).at[1].set(c_ctx)
    cond = cond * jax.nn.sigmoid(cond)
    hm0, sh0 = _state_zeros()
    zero_b = jnp.zeros((1, ada_down.shape[-1]), F32)
    gu_all, gb_all = gate_up.astype(BF16), gate_b.astype(F32)[:, :, None, :]
    wbr_all, wo_all = w_branch.astype(BF16), w_out.astype(BF16)
    wu_all, wd_all = mlp_up.astype(BF16), mlp_down.astype(BF16)
    for l in range(DEPTH):
        last = l == DEPTH - 1
        g = norm_g[l].astype(F32)
        mods = _mm_small(_mm_small(cond, ada_down, zero_b, ada_down.shape[-1], l), ada_up, ada_b[l][None, :], 2048, l)
        sx1, cx1, gx1, sx2, cx2, gx2 = [mods[0:1, i * d:(i + 1) * d] for i in range(6)]
        sc1, cc1, gc1, sc2, cc2, gc2 = [mods[1:2, i * d:(i + 1) * d] for i in range(6)]
        w_in_l = _permute_w_in(w_in, l)
        mamba_p = (conv_w[l], conv_b[l], ssm_dt_bias[l], ssm_a_log[l], ssm_d[l], ssm_norm[l])

        uc = _inproj(cs, g[0:1], sc1, cc1, w_in_l, tm_c, 768)
        ym_c, hm_f, hm_b = _mamba(uc, *mamba_p, hm0, hm0)
        yh_c, sh_f, sh_b = _hgrn(uc, lower_bounds[l], hgrn_norm[l], sh0, sh0, 1)
        ux = _inproj(xs, g[0:1], sx1, cx1, w_in_l, tm_x, 768)
        ym, _, _ = _mamba(ux, *mamba_p, hm_f, hm_b)
        yh, _, _ = _hgrn(ux, lower_bounds[l], hgrn_norm[l], sh_f, sh_b, GRID_W)
        yf = _fnet(ux, GRID_W)
        xs = _merge(ym, yh, yf, ux, gu_all, gb_all, wbr_all, wo_all, l, xs, g[1:2], gx1, tm_x, 512)
        xs = _mlp(xs, g[2:3], sx2, cx2, wu_all, wd_all, l, g[3:4], gx2, tm_x, 512, 512)
        if not last:
            yf_c = _fnet_small(uc)
            cs = _merge(ym_c, yh_c, yf_c, uc, gu_all, gb_all, wbr_all, wo_all, l, cs, g[1:2], gc1, tm_c, 512)
            cs = _mlp(cs, g[2:3], sc2, cc2, wu_all, wd_all, l, g[3:4], gc2, tm_c, 512, 512)
    return xs[None]
```

```python
import functools
import math

import numpy as np
import jax
import jax.numpy as jnp
from jax import lax
from jax.experimental import pallas as pl
from jax.experimental.pallas import tpu as pltpu

F32 = jnp.float32
BF16 = jnp.bfloat16

D_MODEL = 4096
DEPTH = 4
GRID_W = 64
BRANCH_W = 1536
SSM_HEAD_DIM = 64
SSM_HEADS = 24
SSM_GROUPS = 4
SSM_HPG = 6
SSM_STATE = 128
CONV_W = 5
XBC_W = BRANCH_W + 2 * SSM_GROUPS * SSM_STATE
HGRN_VAL = 128
HGRN_HEADS = 12
HGRN_KEY = 64
HGRN_HK = HGRN_HEADS * HGRN_KEY
HGRN_PAIRS = HGRN_HEADS // 2
FOURIER_GROUPS = 4
FOURIER_GW = BRANCH_W // FOURIER_GROUPS
GATE_RANK = 256
D_FF = 4 * D_MODEL
CHUNK = 128
EPS = 1e-6
F_MIN = 1e-30
IN_SPLITS = (BRANCH_W, XBC_W, 2 * SSM_HEADS, HGRN_HK, 2 * HGRN_HK, BRANCH_W, BRANCH_W, BRANCH_W, GATE_RANK)

LANES = 128
SUBLANES = 8
VMEM_LIMIT = 56 * 1024 * 1024

OFF_Z = 0
OFF_FOUR = OFF_Z + BRANCH_W
OFF_XBC = OFF_FOUR + BRANCH_W
OFF_Q = OFF_XBC + XBC_W
OFF_F = OFF_Q + HGRN_HK
OFF_I = OFF_F + 2 * HGRN_HK
OFF_G = OFF_I + BRANCH_W
OFF_GATE = OFF_G + BRANCH_W
OFF_DT = OFF_GATE + GATE_RANK
N_PAD = 11520
NEG_BIG = -1e30
GLA_PP = 6


def _cparams(sem):
    return pltpu.CompilerParams(dimension_semantics=sem, vmem_limit_bytes=VMEM_LIMIT)


def _sigmoid(x):
    return 1.0 / (1.0 + jnp.exp(-x))


def _silu(x):
    return x * _sigmoid(x)


def _softplus(x):
    return jnp.maximum(x, 0.0) + jnp.log1p(jnp.exp(-jnp.abs(x)))


def _split3(x):
    hi = x.astype(BF16)
    r1 = x - hi.astype(F32)
    mid = r1.astype(BF16)
    lo = (r1 - mid.astype(F32)).astype(BF16)
    return hi, mid, lo


def _split2(x):
    hi = x.astype(BF16)
    return hi, (x - hi.astype(F32)).astype(BF16)


def _dot01_l(m01, x):
    hi, mid, lo = _split3(x)
    d = lambda a: jnp.dot(m01, a, preferred_element_type=F32)
    return d(hi) + d(mid) + d(lo)


def _dot_nt(a, b):
    return lax.dot_general(a, b, (((1,), (1,)), ((), ())), preferred_element_type=F32)


def _rms_mod(x, g, shift, scale):
    ms = jnp.mean(x * x, axis=-1, keepdims=True)
    y = (x * lax.rsqrt(ms + EPS)) * g
    return y * (1.0 + scale) + shift


def _mm_small_kernel(a_ref, w_ref, b_ref, o_ref):
    o_ref[...] = jnp.dot(a_ref[...].astype(BF16), w_ref[...].astype(BF16),
                         preferred_element_type=F32) + b_ref[...]


def _mm_small(a, w, b, tn, layer):
    m, k = a.shape
    n = w.shape[2]
    return pl.pallas_call(
        _mm_small_kernel,
        grid=(n // tn,),
        in_specs=[pl.BlockSpec((m, k), lambda j: (0, 0)),
                  pl.BlockSpec((None, k, tn), lambda j: (layer, 0, j)),
                  pl.BlockSpec((1, tn), lambda j: (0, j))],
        out_specs=pl.BlockSpec((m, tn), lambda j: (0, j)),
        out_shape=jax.ShapeDtypeStruct((m, n), F32),
        compiler_params=_cparams(("parallel",)),
        name="mm_small",
    )(a, w, b)


def _inproj_kernel(x_ref, g_ref, sh_ref, sc_ref, w_ref, o_ref, hx_ref):
    @pl.when(pl.program_id(1) == 0)
    def _():
        hx_ref[...] = _rms_mod(x_ref[...], g_ref[...], sh_ref[...], sc_ref[...]).astype(BF16)

    o_ref[...] = jnp.dot(hx_ref[...], w_ref[...], preferred_element_type=F32).astype(o_ref.dtype)


def _inproj(x, g, shift, scale, w, tm, tn):
    m, d = x.shape
    n = w.shape[1]
    row = lambda i, j: (0, 0)
    return pl.pallas_call(
        _inproj_kernel,
        grid=(m // tm, n // tn),
        in_specs=[pl.BlockSpec((tm, d), lambda i, j: (i, 0)),
                  pl.BlockSpec((1, d), row), pl.BlockSpec((1, d), row), pl.BlockSpec((1, d), row),
                  pl.BlockSpec((d, tn), lambda i, j: (0, j))],
        out_specs=pl.BlockSpec((tm, tn), lambda i, j: (i, j)),
        out_shape=jax.ShapeDtypeStruct((m, n), F32),
        scratch_shapes=[pltpu.VMEM((tm, d), BF16)],
        compiler_params=_cparams(("parallel", "arbitrary")),
        name="inproj",
    )(x, g, shift, scale, w)


def _ssd_kernel(*refs, reverse, final, direction):
    if final:
        (xs_ref, b_ref, c_ref, dt_ref, dtb_ref, a_ref, exp_ref, h0_ref,
         yp_ref, z_ref, dsk_ref, nrm_ref, y_ref, hfin_ref, st_ref) = refs
    else:
        (xs_ref, b_ref, c_ref, dt_ref, dtb_ref, a_ref, exp_ref, h0_ref, y_ref, hfin_ref, st_ref) = refs
    c = pl.program_id(0)
    nc = pl.num_programs(0)
    gw = BRANCH_W // SSM_GROUPS

    @pl.when(c == 0)
    def _():
        st_ref[...] = h0_ref[...]

    rows = lax.broadcasted_iota(jnp.int32, (CHUNK, CHUNK), 0)
    cols = lax.broadcasted_iota(jnp.int32, (CHUNK, CHUNK), 1)
    tri = (rows <= cols) if reverse else (rows >= cols)
    tri_bf = tri.astype(BF16)
    lane_lo = cols < SSM_HEAD_DIM
    end_row = 0 if reverse else CHUNK - 1

    dt_all = _softplus(dt_ref[...] + dtb_ref[...])
    dta = dt_all * a_ref[...]
    acum = _dot01_l(tri_bf, dta)
    acum_t = acum.T
    a_hi, a_lo = _split2(acum)
    dt_bf = dt_all.astype(BF16)

    for g in range(SSM_GROUPS):
        exp_m = exp_ref[g]
        acum_b = (jnp.dot(a_hi, exp_m, preferred_element_type=F32)
                  + jnp.dot(a_lo, exp_m, preferred_element_type=F32))
        dt_b = jnp.dot(dt_bf, exp_m, preferred_element_type=F32)
        xs = xs_ref[:, gw * g:gw * (g + 1)].astype(F32)
        bm = b_ref[:, SSM_STATE * g:SSM_STATE * (g + 1)]
        cm = c_ref[:, SSM_STATE * g:SSM_STATE * (g + 1)]
        bm_t = bm.astype(F32).T.astype(BF16)
        scores = _dot_nt(cm, bm)
        ys = []
        for j in range(SSM_HPG // 2):
            r0, r1 = 2 * j, 2 * j + 1
            col0 = direction * SSM_HEADS + g * SSM_HPG + r0
            ab0 = acum_b[:, LANES * r0:LANES * (r0 + 1)]
            ab1 = acum_b[:, LANES * r1:LANES * (r1 + 1)]
            acum_e = jnp.where(lane_lo, ab0, ab1)
            dt_e = jnp.where(lane_lo, dt_b[:, LANES * r0:LANES * (r0 + 1)], dt_b[:, LANES * r1:LANES * (r1 + 1)])
            xp = xs[:, LANES * j:LANES * (j + 1)] * dt_e
            end = acum_e[end_row:end_row + 1, :]
            xw = (xp * jnp.exp(end - acum_e)).astype(BF16)
            l0 = jnp.exp(jnp.where(tri, ab0 - acum_t[col0:col0 + 1, :], NEG_BIG))
            l1 = jnp.exp(jnp.where(tri, ab1 - acum_t[col0 + 1:col0 + 2, :], NEG_BIG))
            m0 = (scores * l0).astype(BF16)
            m1 = (scores * l1).astype(BF16)
            xp0 = jnp.where(lane_lo, xp, 0.0).astype(BF16)
            xp1 = jnp.where(lane_lo, 0.0, xp).astype(BF16)
            y_diag = (jnp.dot(m0, xp0, preferred_element_type=F32)
                      + jnp.dot(m1, xp1, preferred_element_type=F32))
            st = st_ref[g, j]
            y_off = jnp.dot(cm, st.astype(BF16), preferred_element_type=F32) * jnp.exp(acum_e)
            ys.append(y_diag + y_off)
            st_ref[g, j] = st * jnp.exp(end) + jnp.dot(bm_t, xw, preferred_element_type=F32)
        y = jnp.concatenate(ys, axis=1)
        if final:
            gs = slice(gw * g, gw * (g + 1))
            y = y + yp_ref[:, gs] + xs * dsk_ref[:, gs]
            y = y * _silu(z_ref[:, gs])
            y = y * lax.rsqrt(jnp.mean(y * y, axis=-1, keepdims=True) + EPS)
            y = y * nrm_ref[:, gs]
        y_ref[:, gw * g:gw * (g + 1)] = y.astype(y_ref.dtype)

    @pl.when(c == nc - 1)
    def _():
        hfin_ref[...] = st_ref[...]


def _ssd_tables(direction):
    exp_m = np.zeros((SSM_GROUPS, LANES, SSM_HPG * LANES), np.float32)
    for g in range(SSM_GROUPS):
        for r in range(SSM_HPG):
            col = direction * SSM_HEADS + g * SSM_HPG + r
            exp_m[g, col, LANES * r:LANES * (r + 1)] = 1.0
    return jnp.asarray(exp_m, BF16)


def _ssd_pass(xbc, u_all, dtb_row, a_row, h0, direction, final_args=None):
    L = xbc.shape[0]
    nc = L // CHUNK
    reverse = direction == 1
    final = final_args is not None
    exp_m = _ssd_tables(direction)
    bc_w = SSM_GROUPS * SSM_STATE
    cidx = (lambda c: nc - 1 - c) if reverse else (lambda c: c)
    st_shape = (SSM_GROUPS, SSM_HPG // 2, SSM_STATE, LANES)
    in_specs = [
        pl.BlockSpec((CHUNK, BRANCH_W), lambda c: (cidx(c), 0)),
        pl.BlockSpec((CHUNK, bc_w), lambda c: (cidx(c), BRANCH_W // bc_w)),
        pl.BlockSpec((CHUNK, bc_w), lambda c: (cidx(c), BRANCH_W // bc_w + 1)),
        pl.BlockSpec((CHUNK, LANES), lambda c: (cidx(c), OFF_DT // LANES)),
        pl.BlockSpec((1, LANES), lambda c: (0, 0)),
        pl.BlockSpec((1, LANES), lambda c: (0, 0)),
        pl.BlockSpec((SSM_GROUPS, LANES, SSM_HPG * LANES), lambda c: (0, 0, 0)),
        pl.BlockSpec(st_shape, lambda c: (0, 0, 0, 0)),
    ]
    args = [xbc, xbc, xbc, u_all, dtb_row, a_row, exp_m, h0]
    if final:
        y_prev, dsk_row, nrm_row = final_args
        in_specs += [
            pl.BlockSpec((CHUNK, BRANCH_W), lambda c: (cidx(c), 0)),
            pl.BlockSpec((CHUNK, BRANCH_W), lambda c: (cidx(c), OFF_Z // BRANCH_W)),
            pl.BlockSpec((1, BRANCH_W), lambda c: (0, 0)),
            pl.BlockSpec((1, BRANCH_W), lambda c: (0, 0)),
        ]
        args += [y_prev, u_all, dsk_row, nrm_row]
    out_dtype = BF16 if final else F32
    y, hfin = pl.pallas_call(
        functools.partial(_ssd_kernel, reverse=reverse, final=final, direction=direction),
        grid=(nc,),
        in_specs=in_specs,
        out_specs=[pl.BlockSpec((CHUNK, BRANCH_W), lambda c: (cidx(c), 0)),
                   pl.BlockSpec(st_shape, lambda c: (0, 0, 0, 0))],
        out_shape=[jax.ShapeDtypeStruct((L, BRANCH_W), out_dtype),
                   jax.ShapeDtypeStruct(st_shape, F32)],
        scratch_shapes=[pltpu.VMEM(st_shape, F32)],
        compiler_params=_cparams(("arbitrary",)),
        name="ssd_bwd" if reverse else "ssd_fwd",
    )(*args)
    return y, hfin


def _conv_kernel(u_ref, up_ref, un_ref, w_ref, b_ref, o_ref, *, tm):
    i = pl.program_id(0)
    halo = SUBLANES
    n = tm + 2 * halo
    cur = u_ref[...]
    prev = jnp.where(i > 0, up_ref[...], 0.0)
    nxt = jnp.where(i < pl.num_programs(0) - 1, un_ref[...], 0.0)
    ext = jnp.concatenate([prev, cur, nxt], axis=0)
    mid = CONV_W // 2
    acc = b_ref[...] + w_ref[mid:mid + 1, :] * cur
    for k in range(CONV_W):
        d = k - mid
        if d != 0:
            acc = acc + w_ref[k:k + 1, :] * pltpu.roll(ext, (-d) % n, 0)[halo:halo + tm]
    o_ref[...] = (acc * _sigmoid(acc)).astype(o_ref.dtype)


def _conv_silu(u_all, conv_w, conv_b, tm):
    L = u_all.shape[0]
    cw = SSM_GROUPS * SSM_STATE
    hb = tm // SUBLANES
    last = L // SUBLANES - 1
    c0 = OFF_XBC // cw
    return pl.pallas_call(
        functools.partial(_conv_kernel, tm=tm),
        grid=(L // tm, XBC_W // cw),
        in_specs=[pl.BlockSpec((tm, cw), lambda i, j: (i, c0 + j)),
                  pl.BlockSpec((SUBLANES, cw), lambda i, j: (jnp.maximum(i * hb - 1, 0), c0 + j)),
                  pl.BlockSpec((SUBLANES, cw), lambda i, j: (jnp.minimum((i + 1) * hb, last), c0 + j)),
                  pl.BlockSpec((CONV_W, cw), lambda i, j: (0, j)),
                  pl.BlockSpec((1, cw), lambda i, j: (0, j))],
        out_specs=pl.BlockSpec((tm, cw), lambda i, j: (i, j)),
        out_shape=jax.ShapeDtypeStruct((L, XBC_W), BF16),
        compiler_params=_cparams(("parallel", "parallel")),
        name="conv_silu",
    )(u_all, u_all, u_all, conv_w.astype(F32), conv_b.astype(F32)[None, :])


def _pad_row(v, width=LANES):
    return jnp.zeros((1, width), F32).at[0, :v.shape[0]].set(v.astype(F32))


def _mamba(u_all, conv_w, conv_b, dt_bias, a_log, d_skip, ssm_norm, h0_f, h0_b):
    xbc = _conv_silu(u_all, conv_w, conv_b, min(u_all.shape[0], 512))
    dtb_row = _pad_row(dt_bias.reshape(-1))
    a_row = _pad_row(-jnp.exp(a_log.astype(F32)).reshape(-1))
    dsk_row = jnp.repeat(d_skip.astype(F32), SSM_HEAD_DIM)[None, :]
    nrm_row = ssm_norm.astype(F32)[None, :]
    y_f, hf = _ssd_pass(xbc, u_all, dtb_row, a_row, h0_f, 0)
    y, hb = _ssd_pass(xbc, u_all, dtb_row, a_row, h0_b, 1, (y_f, dsk_row, nrm_row))
    return y, hf, hb


def _gla_chunk(uq, uf, v, lb, st_t, ebc, sel_small, bd_mask, reverse):
    Q = CHUNK
    rows = lax.broadcasted_iota(jnp.int32, (Q, LANES), 0)
    lanes = lax.broadcasted_iota(jnp.int32, (Q, LANES), 1)
    r2 = lax.broadcasted_iota(jnp.int32, (Q, Q), 0)
    c2 = lax.broadcasted_iota(jnp.int32, (Q, Q), 1)
    tri_bf = ((r2 <= c2) if reverse else (r2 >= c2)).astype(BF16)
    lane_lo = lanes < HGRN_KEY
    end_row = 0 if reverse else Q - 1

    q = _silu(uq)
    f = lb + (1.0 - lb) * _sigmoid(uf)
    logf = jnp.log(jnp.maximum(f, F_MIN))
    k = (1.0 - lb) * _sigmoid(-uf)
    gc = _dot01_l(tri_bf, logf)

    xr = r2 ^ c2
    att = jnp.dot((q * k).astype(BF16), ebc, preferred_element_type=F32)
    a0 = jnp.where(xr == 0, att[:, :HGRN_VAL], 0.0)
    a1 = jnp.where(xr == 0, att[:, HGRN_VAL:], 0.0)
    b = 1
    lvl = 0
    while b < Q:
        if 2 * b < SUBLANES:
            ref = _dot01_l(sel_small[lvl], gc)
        else:
            pieces = []
            for p0 in range(0, Q, 2 * b):
                row = p0 + b if reverse else p0 + b - 1
                pieces.append(jnp.broadcast_to(gc[row:row + 1, :], (2 * b, LANES)))
            ref = jnp.concatenate(pieces, axis=0) if len(pieces) > 1 else pieces[0]
        hi = (rows & b) != 0
        is_q = (~hi) if reverse else hi
        qt = (q * jnp.exp(jnp.where(is_q, gc - ref, NEG_BIG))).astype(BF16)
        kt = k * jnp.exp(jnp.where(is_q, NEG_BIG, ref - gc))
        kt0 = jnp.where(lane_lo, kt, 0.0).astype(BF16)
        kt1 = jnp.where(lane_lo, 0.0, kt).astype(BF16)
        att = _dot_nt(qt, jnp.concatenate([kt0, kt1], axis=0))
        if 2 * b < Q:
            same = xr < 2 * b
            a0 = jnp.where(same & (xr >= b), att[:, :Q], a0)
            a1 = jnp.where(same & (xr >= b), att[:, Q:], a1)
        else:
            a0 = jnp.where(xr >= b, att[:, :Q], a0)
            a1 = jnp.where(xr >= b, att[:, Q:], a1)
        b *= 2
        lvl += 1
    vb = v.astype(BF16)
    o = jnp.concatenate([jnp.dot(a0.astype(BF16), vb[:, :HGRN_VAL], preferred_element_type=F32),
                         jnp.dot(a1.astype(BF16), vb[:, HGRN_VAL:], preferred_element_type=F32)], axis=1)

    qg = (q * jnp.exp(gc)).astype(BF16)
    o = o + _dot_nt(qg, st_t.astype(BF16))
    end = gc[end_row:end_row + 1, :]
    kd = (k * jnp.exp(end - gc)).astype(BF16)
    st_new = st_t * jnp.exp(end) + jnp.dot(v.T.astype(BF16), kd, preferred_element_type=F32) * bd_mask
    return o, st_new


def _gla_kernel(u_hbm, lb_ref, ebc_ref, sel_ref, s0_ref, o_hbm, sfin_ref, st_ref, bq, bf, bv, bo, isem, osem,
                *, reverse, direction, ncol, nrow_chunks):
    p = pl.program_id(0)
    c = pl.program_id(1)
    col = (ncol - 1 - c) if reverse else c
    nxt = (col - 1) if reverse else (col + 1)
    slot = c % 2
    vw = 2 * HGRN_VAL
    kw2, vw2 = GLA_PP * LANES, GLA_PP * vw
    q_off = pl.multiple_of(OFF_Q + p * kw2, LANES)
    f_off = pl.multiple_of(OFF_F + direction * HGRN_HK + p * kw2, LANES)
    v_off = pl.multiple_of(OFF_I + p * vw2, LANES)
    o_off = pl.multiple_of(p * vw2, LANES)

    def in_copies(col_, slot_):
        return (pltpu.make_async_copy(u_hbm.at[:, col_, pl.ds(q_off, kw2)], bq.at[slot_], isem.at[0, slot_]),
                pltpu.make_async_copy(u_hbm.at[:, col_, pl.ds(f_off, kw2)], bf.at[slot_], isem.at[1, slot_]),
                pltpu.make_async_copy(u_hbm.at[:, col_, pl.ds(v_off, vw2)], bv.at[slot_], isem.at[2, slot_]))

    def out_copy(col_, slot_):
        return pltpu.make_async_copy(bo.at[slot_], o_hbm.at[:, col_, pl.ds(o_off, vw2)], osem.at[slot_])

    @pl.when(c == 0)
    def _():
        st_ref[...] = s0_ref[...]
        for i, cp in enumerate(in_copies(col, slot)):
            cp.start(priority=i // 2)

    for cp in in_copies(col, slot):
        cp.wait()

    @pl.when(c + 1 < ncol)
    def _():
        for i, cp in enumerate(in_copies(nxt, 1 - slot)):
            cp.start(priority=i // 2)

    @pl.when(c >= 2)
    def _():
        out_copy(col, slot).wait()

    r2 = lax.broadcasted_iota(jnp.int32, (vw, LANES), 0)
    c2 = lax.broadcasted_iota(jnp.int32, (vw, LANES), 1)
    bd_mask = ((r2 < HGRN_VAL) == (c2 < HGRN_KEY)).astype(F32)
    ebc = ebc_ref[...]
    sel_small = [sel_ref[i] for i in range(sel_ref.shape[0])]
    order = range(nrow_chunks - 1, -1, -1) if reverse else range(nrow_chunks)
    for rc in order:
        sl = pl.ds(rc * CHUNK, CHUNK)
        for pp in range(GLA_PP):
            ks = slice(LANES * pp, LANES * (pp + 1))
            vs = slice(vw * pp, vw * (pp + 1))
            o, st_new = _gla_chunk(bq[slot, sl, ks], bf[slot, sl, ks], bv[slot, sl, vs], lb_ref[:, ks], st_ref[pp],
                                   ebc, sel_small, bd_mask, reverse)
            st_ref[pp] = st_new
            bo[slot, sl, vs] = o

    out_copy(col, slot).start()

    @pl.when(c == ncol - 1)
    def _():
        out_copy(col, slot).wait()
        if ncol >= 2:
            out_copy(col, 1 - slot).wait()
        sfin_ref[...] = st_ref[...]


def _gla_pass(u3, lb_rows, s0, direction):
    R, Wc, _ = u3.shape
    reverse = direction == 1
    ebc = np.zeros((LANES, 2 * HGRN_VAL), np.float32)
    ebc[:HGRN_KEY, :HGRN_VAL] = 1.0
    ebc[HGRN_KEY:, HGRN_VAL:] = 1.0
    ebc = jnp.asarray(ebc, BF16)
    small = [b for b in (1, 2, 4) if 2 * b < SUBLANES]
    sel = np.zeros((len(small), CHUNK, CHUNK), np.float32)
    for i, b in enumerate(small):
        for r in range(CHUNK):
            p0 = (r // (2 * b)) * 2 * b
            sel[i, r, p0 + b if reverse else p0 + b - 1] = 1.0
    sel = jnp.asarray(sel, BF16)
    vw = 2 * HGRN_VAL
    o, sfin = pl.pallas_call(
        functools.partial(_gla_kernel, reverse=reverse, direction=direction, ncol=Wc, nrow_chunks=R // CHUNK),
        grid=(HGRN_PAIRS // GLA_PP, Wc),
        in_specs=[pl.BlockSpec(memory_space=pl.ANY),
                  pl.BlockSpec((None, 1, GLA_PP * LANES), lambda p, c: (p, 0, 0)),
                  pl.BlockSpec((LANES, vw), lambda p, c: (0, 0)),
                  pl.BlockSpec((len(small), CHUNK, CHUNK), lambda p, c: (0, 0, 0)),
                  pl.BlockSpec((GLA_PP, vw, LANES), lambda p, c: (p, 0, 0))],
        out_specs=[pl.BlockSpec(memory_space=pl.ANY),
                   pl.BlockSpec((GLA_PP, vw, LANES), lambda p, c: (p, 0, 0))],
        out_shape=[jax.ShapeDtypeStruct((R, Wc, BRANCH_W), F32),
                   jax.ShapeDtypeStruct((HGRN_PAIRS, vw, LANES), F32)],
        scratch_shapes=[pltpu.VMEM((GLA_PP, vw, LANES), F32),
                        pltpu.VMEM((2, R, GLA_PP * LANES), F32), pltpu.VMEM((2, R, GLA_PP * LANES), F32),
                        pltpu.VMEM((2, R, GLA_PP * vw), F32), pltpu.VMEM((2, R, GLA_PP * vw), F32),
                        pltpu.SemaphoreType.DMA((3, 2)), pltpu.SemaphoreType.DMA((2,))],
        compiler_params=_cparams(("arbitrary", "arbitrary")),
        name="gla_bwd" if reverse else "gla_fwd",
    )(u3, lb_rows, ebc, sel, s0)
    return o, sfin


def _hgrn_post_kernel(of_ref, ob_ref, g_ref, nrm_ref, y_ref):
    o = of_ref[...] + ob_ref[...]
    gate = _silu(g_ref[...])
    nrm = nrm_ref[...]
    for h in range(2):
        sl = slice(HGRN_VAL * h, HGRN_VAL * (h + 1))
        oh = o[:, sl]
        oh = oh * lax.rsqrt(jnp.mean(oh * oh, axis=-1, keepdims=True) + EPS)
        y_ref[:, sl] = (oh * nrm * gate[:, sl]).astype(y_ref.dtype)


def _hgrn_post(o_f, o_b, u_all, nrm_row, tm):
    L = o_f.shape[0]
    vw = 2 * HGRN_VAL
    blk = pl.BlockSpec((tm, vw), lambda i, p: (i, p))
    return pl.pallas_call(
        _hgrn_post_kernel,
        grid=(L // tm, HGRN_PAIRS),
        in_specs=[blk, blk, pl.BlockSpec((tm, vw), lambda i, p: (i, OFF_G // vw + p)),
                  pl.BlockSpec((1, HGRN_VAL), lambda i, p: (0, 0))],
        out_specs=blk,
        out_shape=jax.ShapeDtypeStruct((L, BRANCH_W), BF16),
        compiler_params=_cparams(("parallel", "parallel")),
        name="hgrn_post",
    )(o_f, o_b, u_all, nrm_row)


def _hgrn(u_all, lower_bound, hgrn_norm, s0_f, s0_b, grid_w):
    L = u_all.shape[0]
    R = L // grid_w
    u3 = u_all.reshape(R, grid_w, N_PAD)
    lb = lower_bound.astype(F32).reshape(2, HGRN_PAIRS // GLA_PP, 1, GLA_PP * LANES)
    o_f, sf = _gla_pass(u3, lb[0], s0_f, 0)
    o_b, sb = _gla_pass(u3, lb[1], s0_b, 1)
    y = _hgrn_post(o_f.reshape(L, BRANCH_W), o_b.reshape(L, BRANCH_W), u_all, hgrn_norm.astype(F32)[None, :],
                   min(L, 1024))
    return y, sf, sb


def _dft_tables(R, Wc):
    L = R * Wc
    q = np.arange(R)[None, :, None]
    r = np.arange(R)[None, None, :]
    c = np.arange(Wc)[:, None, None]
    ang = 2.0 * np.pi * ((q * (Wc * r + c)) % L) / L
    m = np.concatenate([np.cos(ang), -np.sin(ang)], axis=1)
    qc = np.arange(Wc)[:, None]
    cc = np.arange(Wc)[None, :]
    a2 = 2.0 * np.pi * ((qc * cc) % Wc) / Wc
    t = np.block([[np.cos(a2), np.sin(a2)], [np.sin(a2), -np.cos(a2)]])
    ch = np.arange(FOURIER_GW)
    a3 = 2.0 * np.pi * ((ch[:, None] * ch[None, :]) % FOURIER_GW) / FOURIER_GW
    cs = np.concatenate([np.cos(a3), -np.sin(a3)], axis=0) / math.sqrt(L * FOURIER_GW)
    return (jnp.asarray(m, F32).astype(BF16), jnp.asarray(t, F32).astype(BF16), jnp.asarray(cs, F32).astype(BF16))


def _fnet_a_kernel(x_ref, m_ref, y_ref, *, wb, R):
    for jj in range(wb):
        res = jnp.dot(m_ref[jj], x_ref[:, jj, :].astype(BF16), preferred_element_type=F32)
        y_ref[:, 0, jj, :] = res[:R]
        y_ref[:, 1, jj, :] = res[R:]


def _fnet_b_kernel(y_ref, t_ref, cs_ref, o_ref, *, Wc, rb):
    t = t_ref[...]
    cs = cs_ref[...]
    for i in range(rb):
        yy = y_ref[i].reshape(2 * Wc, FOURIER_GW).astype(BF16)
        pq = jnp.dot(t, yy, preferred_element_type=F32)
        lhs = jnp.concatenate([pq[:Wc], pq[Wc:]], axis=1).astype(BF16)
        o_ref[:, i, :] = jnp.dot(lhs, cs, preferred_element_type=F32).astype(o_ref.dtype)


def _fnet(u_all, grid_w):
    L = u_all.shape[0]
    R = L // grid_w
    Wc = grid_w
    wb = SUBLANES
    rb = SUBLANES
    m, t, cs = _dft_tables(R, Wc)
    u3 = u_all.reshape(R, Wc, N_PAD)
    y = pl.pallas_call(
        functools.partial(_fnet_a_kernel, wb=wb, R=R),
        grid=(Wc // wb, FOURIER_GROUPS),
        in_specs=[pl.BlockSpec((R, wb, FOURIER_GW), lambda cb, g: (0, cb, OFF_FOUR // FOURIER_GW + g)),
                  pl.BlockSpec((wb, 2 * R, R), lambda cb, g: (cb, 0, 0))],
        out_specs=pl.BlockSpec((R, 2, wb, FOURIER_GW), lambda cb, g: (0, 0, cb, g)),
        out_shape=jax.ShapeDtypeStruct((R, 2, Wc, BRANCH_W), F32),
        compiler_params=_cparams(("parallel", "arbitrary")),
        name="fnet_a",
    )(u3, m)
    out = pl.pallas_call(
        functools.partial(_fnet_b_kernel, Wc=Wc, rb=rb),
        grid=(R // rb, FOURIER_GROUPS),
        in_specs=[pl.BlockSpec((rb, 2, Wc, FOURIER_GW), lambda qb, g: (qb, 0, 0, g)),
                  pl.BlockSpec((2 * Wc, 2 * Wc), lambda qb, g: (0, 0)),
                  pl.BlockSpec((2 * FOURIER_GW, FOURIER_GW), lambda qb, g: (0, 0))],
        out_specs=pl.BlockSpec((Wc, rb, FOURIER_GW), lambda qb, g: (0, qb, g)),
        out_shape=jax.ShapeDtypeStruct((Wc, R, BRANCH_W), BF16),
        compiler_params=_cparams(("parallel", "arbitrary")),
        name="fnet_b",
    )(y, t, cs)
    return out.reshape(L, BRANCH_W)


def _fnet_small_kernel(x_ref, m_ref, cs_ref, o_ref, *, L):
    pq = jnp.dot(m_ref[...], x_ref[...].astype(BF16), preferred_element_type=F32)
    lhs = jnp.concatenate([pq[:L], -pq[L:]], axis=1).astype(BF16)
    o_ref[...] = jnp.dot(lhs, cs_ref[...], preferred_element_type=F32).astype(o_ref.dtype)


def _fnet_small(u_all):
    L = u_all.shape[0]
    m, _, cs = _dft_tables(L, 1)
    return pl.pallas_call(
        functools.partial(_fnet_small_kernel, L=L),
        grid=(FOURIER_GROUPS,),
        in_specs=[pl.BlockSpec((L, FOURIER_GW), lambda g: (0, OFF_FOUR // FOURIER_GW + g)),
                  pl.BlockSpec((2 * L, L), lambda g: (0, 0)),
                  pl.BlockSpec((2 * FOURIER_GW, FOURIER_GW), lambda g: (0, 0))],
        out_specs=pl.BlockSpec((L, FOURIER_GW), lambda g: (0, g)),
        out_shape=jax.ShapeDtypeStruct((L, BRANCH_W), BF16),
        compiler_params=_cparams(("parallel",)),
        name="fnet_small",
    )(u_all, m[0], cs)


def _merge_kernel(ym_ref, yh_ref, yf_ref, ug_ref, gu_ref, gb_ref, wbr_ref, wo_ref, x_ref, g1_ref, gt_ref,
                  o_ref, mg_ref, acc_ref, ss_ref, *, nb, tn):
    j = pl.program_id(1)

    @pl.when(j < nb)
    def _():
        ug = ug_ref[...].astype(BF16)
        tot = None
        for i, y_ref in enumerate((ym_ref, yh_ref, yf_ref)):
            gate = _sigmoid(jnp.dot(ug, gu_ref[i], preferred_element_type=F32) + gb_ref[i])
            term = gate * jnp.dot(y_ref[...], wbr_ref[i], preferred_element_type=F32)
            tot = term if tot is None else tot + term
        mg_ref[j] = tot.astype(BF16)

    @pl.when((j >= nb) & (j < 2 * nb))
    def _():
        acc = jnp.dot(mg_ref[0], wo_ref[0:tn, :], preferred_element_type=F32)
        for kk in range(1, nb):
            acc = acc + jnp.dot(mg_ref[kk], wo_ref[kk * tn:(kk + 1) * tn, :], preferred_element_type=F32)
        acc_ref[j - nb] = acc
        part = jnp.sum(acc * acc, axis=-1, keepdims=True)

        @pl.when(j == nb)
        def _():
            ss_ref[...] = part

        @pl.when(j > nb)
        def _():
            ss_ref[...] = ss_ref[...] + part

    @pl.when(j >= 2 * nb)
    def _():
        inv = lax.rsqrt(ss_ref[...] * (1.0 / (nb * tn)) + EPS)
        o_ref[...] = x_ref[...] + gt_ref[...] * ((acc_ref[j - 2 * nb] * inv) * g1_ref[...])


def _merge(ym, yh, yf, u_all, gate_up, gate_b, w_branch, w_out, layer, x, g1, gate, tm, tn):
    m, d = x.shape
    nb = d // tn
    p1 = lambda i, j: jnp.minimum(j, nb - 1)
    p2 = lambda i, j: jnp.clip(j - nb, 0, nb - 1)
    p3 = lambda i, j: jnp.clip(j - 2 * nb, 0, nb - 1)
    ybs = pl.BlockSpec((tm, BRANCH_W), lambda i, j: (i, 0))
    return pl.pallas_call(
        functools.partial(_merge_kernel, nb=nb, tn=tn),
        grid=(m // tm, 3 * nb),
        in_specs=[ybs, ybs, ybs,
                  pl.BlockSpec((tm, GATE_RANK), lambda i, j: (i, OFF_GATE // GATE_RANK)),
                  pl.BlockSpec((None, 3, GATE_RANK, tn), lambda i, j: (layer, 0, 0, p1(i, j))),
                  pl.BlockSpec((None, 3, 1, tn), lambda i, j: (layer, 0, 0, p1(i, j))),
                  pl.BlockSpec((None, 3, BRANCH_W, tn), lambda i, j: (layer, 0, 0, p1(i, j))),
                  pl.BlockSpec((None, d, tn), lambda i, j: (layer, 0, p2(i, j))),
                  pl.BlockSpec((tm, tn), lambda i, j: (i, p3(i, j))),
                  pl.BlockSpec((1, tn), lambda i, j: (0, p3(i, j))),
                  pl.BlockSpec((1, tn), lambda i, j: (0, p3(i, j)))],
        out_specs=pl.BlockSpec((tm, tn), lambda i, j: (i, p3(i, j))),
        out_shape=jax.ShapeDtypeStruct((m, d), F32),
        scratch_shapes=[pltpu.VMEM((nb, tm, tn), BF16), pltpu.VMEM((nb, tm, tn), F32), pltpu.VMEM((tm, 1), F32)],
        compiler_params=_cparams(("parallel", "arbitrary")),
        name="merge",
    )(ym, yh, yf, u_all, gate_up, gate_b, w_branch, w_out, x, g1, gate)


def _mlp_kernel(x_ref, g_ref, sh_ref, sc_ref, wu_ref, wd_ref, xb_ref, g3_ref, gt_ref, o_ref,
                hx_ref, h0_ref, h1_ref, acc_ref, inv_ref, *, nh, nb, tn):
    j = pl.program_id(1)

    def up(h_ref):
        h = jnp.dot(hx_ref[...], wu_ref[...], preferred_element_type=F32)
        h_ref[...] = jnp.square(jnp.maximum(h, 0.0)).astype(BF16)

    def down(h_ref):
        h = h_ref[...]
        for kk in range(nb):
            acc_ref[kk] = acc_ref[kk] + jnp.dot(h, wd_ref[:, kk * tn:(kk + 1) * tn], preferred_element_type=F32)

    @pl.when(j == 0)
    def _():
        hx_ref[...] = _rms_mod(x_ref[...], g_ref[...], sh_ref[...], sc_ref[...]).astype(BF16)
        acc_ref[...] = jnp.zeros(acc_ref.shape, F32)
        up(h0_ref)

    @pl.when((j > 0) & (j < nh) & (j % 2 == 1))
    def _():
        down(h0_ref)
        up(h1_ref)

    @pl.when((j > 0) & (j < nh) & (j % 2 == 0))
    def _():
        down(h1_ref)
        up(h0_ref)

    @pl.when(j == nh)
    def _():
        down(h1_ref if (nh - 1) % 2 == 1 else h0_ref)
        ss = jnp.sum(jnp.square(acc_ref[0]), axis=-1, keepdims=True)
        for kk in range(1, nb):
            ss = ss + jnp.sum(jnp.square(acc_ref[kk]), axis=-1, keepdims=True)
        inv_ref[...] = lax.rsqrt(ss * (1.0 / (nb * tn)) + EPS)

    @pl.when(j > nh)
    def _():
        o_ref[...] = xb_ref[...] + gt_ref[...] * ((acc_ref[j - nh - 1] * inv_ref[...]) * g3_ref[...])


def _mlp(x, g2, shift, scale, w_up, w_down, layer, g3, gate, tm, th, tn):
    m, d = x.shape
    dff = w_up.shape[2]
    nh = dff // th
    nb = d // tn
    row = lambda i, j: (0, 0)
    ph = lambda i, j: jnp.minimum(j, nh - 1)
    pd = lambda i, j: jnp.clip(j - 1, 0, nh - 1)
    pe = lambda i, j: jnp.clip(j - nh - 1, 0, nb - 1)
    return pl.pallas_call(
        functools.partial(_mlp_kernel, nh=nh, nb=nb, tn=tn),
        grid=(m // tm, nh + 1 + nb),
        in_specs=[pl.BlockSpec((tm, d), lambda i, j: (i, 0), pipeline_mode=pl.Buffered(1)),
                  pl.BlockSpec((1, d), row), pl.BlockSpec((1, d), row), pl.BlockSpec((1, d), row),
                  pl.BlockSpec((None, d, th), lambda i, j: (layer, 0, ph(i, j))),
                  pl.BlockSpec((None, th, d), lambda i, j: (layer, pd(i, j), 0)),
                  pl.BlockSpec((tm, tn), lambda i, j: (i, pe(i, j))),
                  pl.BlockSpec((1, tn), lambda i, j: (0, pe(i, j))),
                  pl.BlockSpec((1, tn), lambda i, j: (0, pe(i, j)))],
        out_specs=pl.BlockSpec((tm, tn), lambda i, j: (i, pe(i, j))),
        out_shape=jax.ShapeDtypeStruct((m, d), F32),
        scratch_shapes=[pltpu.VMEM((tm, d), BF16), pltpu.VMEM((tm, th), BF16), pltpu.VMEM((tm, th), BF16),
                        pltpu.VMEM((nb, tm, tn), F32), pltpu.VMEM((tm, 1), F32)],
        compiler_params=_cparams(("parallel", "arbitrary")),
        name="mlp",
    )(x, g2, shift, scale, w_up, w_down, x, g3, gate)


_DT_SHIFT = IN_SPLITS[2]


def _wprep_table():
    src, mode = [], []
    nblk = lambda w: w // LANES
    add = lambda a0, n, m: (src.extend(range(a0, a0 + n)), mode.extend([m] * n))
    add(0, nblk(BRANCH_W), 1)
    add((sum(IN_SPLITS[:7]) - _DT_SHIFT) // LANES, nblk(BRANCH_W), 0)
    add(BRANCH_W // LANES, nblk(XBC_W), 1)
    add((BRANCH_W + XBC_W) // LANES, nblk(HGRN_HK + 2 * HGRN_HK + 2 * BRANCH_W), 0)
    add((sum(IN_SPLITS[:8]) - _DT_SHIFT) // LANES, nblk(GATE_RANK), 0)
    add((BRANCH_W + XBC_W) // LANES, 1, 2)
    while len(src) < N_PAD // LANES:
        add(0, 1, 3)
    return np.asarray(src, np.int32), np.asarray(mode, np.int32)


def _wprep_kernel(src_ref, mode_ref, a_ref, b_ref, o_ref):
    mode = mode_ref[pl.program_id(0)]
    lanes = lax.broadcasted_iota(jnp.int32, a_ref.shape, 1)

    @pl.when(mode == 0)
    def _():
        ra = pltpu.roll(a_ref[...], LANES - _DT_SHIFT, 1)
        rb = pltpu.roll(b_ref[...], LANES - _DT_SHIFT, 1)
        o_ref[...] = jnp.where(lanes < LANES - _DT_SHIFT, ra, rb).astype(o_ref.dtype)

    @pl.when(mode == 1)
    def _():
        o_ref[...] = a_ref[...].astype(o_ref.dtype)

    @pl.when(mode == 2)
    def _():
        o_ref[...] = jnp.where(lanes < _DT_SHIFT, a_ref[...], 0.0).astype(o_ref.dtype)

    @pl.when(mode == 3)
    def _():
        o_ref[...] = jnp.zeros(o_ref.shape, o_ref.dtype)


def _permute_w_in(w, layer):
    _, d, n = w.shape
    src, mode = _wprep_table()
    last = (n - 1) // LANES
    return pl.pallas_call(
        _wprep_kernel,
        grid_spec=pltpu.PrefetchScalarGridSpec(
            num_scalar_prefetch=2,
            grid=(N_PAD // LANES,),
            in_specs=[pl.BlockSpec((None, d, LANES), lambda j, src, mode: (layer, 0, src[j])),
                      pl.BlockSpec((None, d, LANES), lambda j, src, mode: (layer, 0, jnp.minimum(src[j] + 1, last)))],
            out_specs=pl.BlockSpec((d, LANES), lambda j, src, mode: (0, j))),
        out_shape=jax.ShapeDtypeStruct((d, N_PAD), BF16),
        compiler_params=_cparams(("arbitrary",)),
        name="wprep",
    )(jnp.asarray(src), jnp.asarray(mode), w, w)


def _state_zeros():
    hm0 = jnp.zeros((SSM_GROUPS, SSM_HPG // 2, SSM_STATE, LANES), F32)
    sh0 = jnp.zeros((HGRN_PAIRS, 2 * HGRN_VAL, LANES), F32)
    return hm0, sh0


def kernel(x, c, ctx, c_ctx, ada_down, ada_up, ada_b, norm_g, w_in, conv_w, conv_b, ssm_dt_bias, ssm_a_log,
           ssm_d, ssm_norm, hgrn_lb_logits, hgrn_norm, gate_up, gate_b, w_branch, w_out, mlp_up, mlp_down):
    assert x.shape[0] == 1 and ctx.shape[0] == 1
    xs = x[0]
    cs = ctx[0]
    L, d = xs.shape
    lc = cs.shape[0]
    tm_x, tm_c = 512, lc
    p_lb = jax.nn.softmax(hgrn_lb_logits.astype(F32), axis=0)
    lower_bounds = jnp.clip(jnp.cumsum(p_lb, axis=0) - p_lb[0], 0.0, 1.0)
    cond = jnp.zeros((SUBLANES, d), F32).at[0].set(c[0]).at[1].set(c_ctx)
    cond = cond * jax.nn.sigmoid(cond)
    hm0, sh0 = _state_zeros()
    zero_b = jnp.zeros((1, ada_down.shape[-1]), F32)
    gu_all, gb_all = gate_up.astype(BF16), gate_b.astype(F32)[:, :, None, :]
    wbr_all, wo_all = w_branch.astype(BF16), w_out.astype(BF16)
    wu_all, wd_all = mlp_up.astype(BF16), mlp_down.astype(BF16)
    for l in range(DEPTH):
        last = l == DEPTH - 1
        g = norm_g[l].astype(F32)
        mods = _mm_small(_mm_small(cond, ada_down, zero_b, ada_down.shape[-1], l), ada_up, ada_b[l][None, :], 2048, l)
        sx1, cx1, gx1, sx2, cx2, gx2 = [mods[0:1, i * d:(i + 1) * d] for i in range(6)]
        sc1, cc1, gc1, sc2, cc2, gc2 = [mods[1:2, i * d:(i + 1) * d] for i in range(6)]
        w_in_l = _permute_w_in(w_in, l)
        mamba_p = (conv_w[l], conv_b[l], ssm_dt_bias[l], ssm_a_log[l], ssm_d[l], ssm_norm[l])

        uc = _inproj(cs, g[0:1], sc1, cc1, w_in_l, tm_c, 768)
        ym_c, hm_f, hm_b = _mamba(uc, *mamba_p, hm0, hm0)
        yh_c, sh_f, sh_b = _hgrn(uc, lower_bounds[l], hgrn_norm[l], sh0, sh0, 1)
        ux = _inproj(xs, g[0:1], sx1, cx1, w_in_l, tm_x, 768)
        ym, _, _ = _mamba(ux, *mamba_p, hm_f, hm_b)
        yh, _, _ = _hgrn(ux, lower_bounds[l], hgrn_norm[l], sh_f, sh_b, GRID_W)
        yf = _fnet(ux, GRID_W)
        xs = _merge(ym, yh, yf, ux, gu_all, gb_all, wbr_all, wo_all, l, xs, g[1:2], gx1, tm_x, 512)
        xs = _mlp(xs, g[2:3], sx2, cx2, wu_all, wd_all, l, g[3:4], gx2, tm_x, 512, 512)
        if not last:
            yf_c = _fnet_small(uc)
            cs = _merge(ym_c, yh_c, yf_c, uc, gu_all, gb_all, wbr_all, wo_all, l, cs, g[1:2], gc1, tm_c, 512)
            cs = _mlp(cs, g[2:3], sc2, cc2, wu_all, wd_all, l, g[3:4], gc2, tm_c, 512, 512)
    return xs[None]
```
